```python
import jax, jax.numpy as jnp
from jax import lax
import numpy as np

D_MODEL = 2048
BATCH = 1
SEQ = 8192
DEPTH = 2
DEC_BATCH = 32
DEC_SEQ = 4
PAST_LEN = 8192
PAGE_SIZE = 128

HEAD_DIM = 128
SB_HEADS = D_MODEL // 256
FOX_HEADS = D_MODEL // 256
GM_GROUPS = D_MODEL // 256
GM_GROUP_W = 128
GM_CHUNK = 128
SB_W = SB_HEADS * HEAD_DIM
FOX_W = FOX_HEADS * HEAD_DIM
GM_W = GM_GROUPS * GM_GROUP_W
MEM_TOKENS = 256
MEM_HEADS = 4
MEM_HEAD_DIM = 128
MEM_W = MEM_HEADS * MEM_HEAD_DIM
D_FF = -((-8 * D_MODEL) // (3 * 256)) * 256
BLOCK_Q = 128
ALPHA = (2.0 * DEPTH) ** 0.25
BETA = (8.0 * DEPTH) ** -0.25
LN_EPS = 1e-5
ATTN_SCALE = HEAD_DIM ** -0.5
MEM_SCALE = MEM_HEAD_DIM ** -0.5
IN_SIZES = (GM_W, GM_W, SB_W, SB_W, SB_W, FOX_W, FOX_W, FOX_W, FOX_HEADS, D_MODEL, D_MODEL, D_MODEL)
N_IN = sum(IN_SIZES)

kernel_name = 'hybrid_gmlp_stickbreak_fox_decoder_step'


def _layernorm(x, g, b):
    xf = x.astype(jnp.float32)
    mu = jnp.mean(xf, axis=-1, keepdims=True)
    var = jnp.mean(jnp.square(xf - mu), axis=-1, keepdims=True)
    y = (xf - mu) * lax.rsqrt(var + LN_EPS) * g.astype(jnp.float32) + b.astype(jnp.float32)
    return y.astype(x.dtype)


def _in_proj(x, w_in_l, b_forget_l):
    B, T, _ = x.shape
    splits = [int(i) for i in np.cumsum(IN_SIZES)[:-1]]
    gu, gv, sq, sk, sv, fq, fk, fv, fg, g_gm, g_sb, g_fox = jnp.split(x @ w_in_l, splits, axis=-1)
    hs = lambda t: t.reshape(B, T, -1, HEAD_DIM)
    logf = jax.nn.log_sigmoid(fg.astype(jnp.float32) + b_forget_l.astype(jnp.float32))
    return gu, gv, hs(sq), hs(sk), hs(sv), hs(fq), hs(fk), hs(fv), logf, (g_gm, g_sb, g_fox)


def _scores(q, k):
    return jnp.einsum('bqhd,bkhd->bhqk', q, k, preferred_element_type=jnp.float32) * ATTN_SCALE


def _apply(w, v):
    return jnp.einsum('bhqk,bkhd->bqhd', w.astype(v.dtype), v)


def _sb_weights(z, mask):
    log_keep = jnp.where(mask, jax.nn.log_sigmoid(-z), 0.0)
    after = lax.cumsum(log_keep, axis=z.ndim - 1, reverse=True) - log_keep
    return jnp.where(mask, jnp.exp(jax.nn.log_sigmoid(z) + after), 0.0)


def _fox_weights(z, c_q, c_k, mask):
    logits = z + c_q[..., :, None] - c_k[..., None, :]
    return jax.nn.softmax(jnp.where(mask, logits, -jnp.inf), axis=-1)


def _sb_prompt(q, k, v):
    B, S, H, Dh = q.shape
    nb = S // BLOCK_Q
    q_blocks = q.reshape(B, nb, BLOCK_Q, H, Dh).swapaxes(0, 1)
    k_pos = jnp.arange(S)

    def block(args):
        q_i, i = args
        q_pos = i * BLOCK_Q + jnp.arange(BLOCK_Q)
        w = _sb_weights(_scores(q_i, k), k_pos[None, :] < q_pos[:, None])
        return _apply(w, v)

    out = lax.map(block, (q_blocks, jnp.arange(nb)))
    return out.swapaxes(0, 1).reshape(B, S, H * Dh)


def _fox_prompt(q, k, v, cum):
    B, S, H, Dh = q.shape
    nb = S // BLOCK_Q
    q_blocks = q.reshape(B, nb, BLOCK_Q, H, Dh).swapaxes(0, 1)
    c_k = cum.swapaxes(1, 2)
    c_blocks = c_k.reshape(B, H, nb, BLOCK_Q).transpose(2, 0, 1, 3)
    k_pos = jnp.arange(S)

    def block(args):
        q_i, c_i, i = args
        q_pos = i * BLOCK_Q + jnp.arange(BLOCK_Q)
        w = _fox_weights(_scores(q_i, k), c_i, c_k, k_pos[None, :] <= q_pos[:, None])
        return _apply(w, v)

    out = lax.map(block, (q_blocks, c_blocks, jnp.arange(nb)))
    return out.swapaxes(0, 1).reshape(B, S, H * Dh)


def _sb_sample(q, k_past, v_past, k_new, v_new):
    B, T, H, Dh = q.shape
    P = k_past.shape[1]
    z = jnp.concatenate([_scores(q, k_past), _scores(q, k_new)], axis=-1)
    q_pos = P + jnp.arange(T)
    k_pos = jnp.arange(P + T)
    w = _sb_weights(z, k_pos[None, :] < q_pos[:, None])
    out = _apply(w[..., :P], v_past) + _apply(w[..., P:], v_new)
    return out.reshape(B, T, H * Dh)


def _fox_sample(q, k_past, v_past, k_new, v_new, logf_past, logf_new):
    B, T, H, Dh = q.shape
    P = k_past.shape[1]
    cum = jnp.cumsum(jnp.concatenate([logf_past, logf_new], axis=1), axis=1).swapaxes(1, 2)
    z = jnp.concatenate([_scores(q, k_past), _scores(q, k_new)], axis=-1)
    q_pos = P + jnp.arange(T)
    k_pos = jnp.arange(P + T)
    w = _fox_weights(z, cum[..., P:], cum, k_pos[None, :] <= q_pos[:, None])
    out = _apply(w[..., :P], v_past) + _apply(w[..., P:], v_new)
    return out.reshape(B, T, H * Dh)


def _gmlp(gu, gv, ln_g, ln_b, w_s, b_s):
    B, T, _ = gu.shape
    c = min(T, GM_CHUNK)
    u = jax.nn.gelu(gu).reshape(B, T // c, c, GM_GROUPS, GM_GROUP_W)
    v = _layernorm(jax.nn.gelu(gv), ln_g, ln_b)
    vc = v.reshape(B, T // c, c, GM_GROUPS, GM_GROUP_W)
    w = jnp.tril(w_s[:, :c, :c])
    mixed = jnp.einsum('gts,bnsgc->bntgc', w, vc) + b_s[:, :c].T[None, None, :, :, None]
    return (u * mixed).reshape(B, T, GM_W), v


def _merge(gates, y_gm, y_sb, y_fx, w_gm, w_sb, w_fx, w_out):
    g_gm, g_sb, g_fx = gates
    h = (jax.nn.sigmoid(g_gm) * (y_gm @ w_gm)
         + jax.nn.sigmoid(g_sb) * (y_sb @ w_sb)
         + jax.nn.sigmoid(g_fx) * (y_fx @ w_fx))
    return h @ w_out


def _mem_attend(x, mk, mv, w_q, w_out):
    B, T, _ = x.shape
    q = (x @ w_q).reshape(B, T, MEM_HEADS, MEM_HEAD_DIM)
    s = jnp.einsum('bqhd,bkhd->bhqk', q, mk, preferred_element_type=jnp.float32) * MEM_SCALE
    o = _apply(jax.nn.softmax(s, axis=-1), mv)
    return o.reshape(B, T, MEM_W) @ w_out


def _swiglu(x, w_gate, w_up, w_down):
    return (jax.nn.silu(x @ w_gate) * (x @ w_up)) @ w_down


def _gather(pool, layer, page_table):
    g = pool[layer, page_table]
    return g.reshape((g.shape[0], g.shape[1] * g.shape[2]) + g.shape[3:])


def setup_inputs(seed: int = 0) -> dict:
    key = jax.random.key(seed)
    ks = iter(jax.random.split(key, 64))

    def nrm(shape, scale):
        return jax.random.normal(next(ks), shape, jnp.float32) * scale

    n_pages = PAST_LEN // PAGE_SIZE
    n_used = DEC_BATCH * n_pages
    n_pool = n_used + max(1, n_used // 4)
    perm = jax.random.permutation(next(ks), n_pool)
    page_table = perm[:n_used].reshape(DEC_BATCH, n_pages).astype(jnp.int32)
    L = DEPTH
    d = D_MODEL
    kv_sb = (L, n_pool, PAGE_SIZE, SB_HEADS, HEAD_DIM)
    kv_fx = (L, n_pool, PAGE_SIZE, FOX_HEADS, HEAD_DIM)
    mem_kv = (L, DEC_BATCH, MEM_TOKENS, MEM_HEADS, MEM_HEAD_DIM)
    return {
        'x_prompt': nrm((BATCH, SEQ, d), 1.0),
        'x_sample': nrm((DEC_BATCH, DEC_SEQ, d), 1.0),
        'cache_sb_k': nrm(kv_sb, 1.0),
        'cache_sb_v': nrm(kv_sb, 1.0),
        'cache_fox_k': nrm(kv_fx, 1.0),
        'cache_fox_v': nrm(kv_fx, 1.0),
        'cache_fox_logf': jax.nn.log_sigmoid(2.0 + nrm((L, n_pool, PAGE_SIZE, FOX_HEADS), 1.0)),
        'cache_mem_k': nrm(mem_kv, 1.0),
        'cache_mem_v': nrm(mem_kv, 1.0),
        'page_table': page_table,
        'mem_prompt': nrm((BATCH, MEM_TOKENS, d), 1.0),
        'w_in': nrm((L, d, N_IN), d ** -0.5),
        'b_forget': 2.0 + nrm((L, FOX_HEADS), 0.5),
        'gm_ln_g': 1.0 + nrm((L, GM_W), 0.02),
        'gm_ln_b': nrm((L, GM_W), 0.02),
        'gm_w_s': nrm((L, GM_GROUPS, GM_CHUNK, GM_CHUNK), 0.5 * GM_CHUNK ** -0.5),
        'gm_b_s': 1.0 + nrm((L, GM_GROUPS, GM_CHUNK), 0.02),
        'w_br_gm': nrm((L, GM_W, d), BETA * GM_W ** -0.5),
        'w_br_sb': nrm((L, SB_W, d), BETA * SB_W ** -0.5),
        'w_br_fox': nrm((L, FOX_W, d), BETA * FOX_W ** -0.5),
        'w_mix_out': nrm((L, d, d), BETA * d ** -0.5),
        'ln1_g': 1.0 + nrm((L, d), 0.02),
        'ln1_b': nrm((L, d), 0.02),
        'mem_w_q': nrm((L, d, MEM_W), d ** -0.5),
        'mem_w_k': nrm((L, d, MEM_W), d ** -0.5),
        'mem_w_v': nrm((L, d, MEM_W), BETA * d ** -0.5),
        'mem_w_out': nrm((L, MEM_W, d), BETA * MEM_W ** -0.5),
        'ln2_g': 1.0 + nrm((L, d), 0.02),
        'ln2_b': nrm((L, d), 0.02),
        'ffn_w_gate': nrm((L, d, D_FF), d ** -0.5),
        'ffn_w_up': nrm((L, d, D_FF), BETA * d ** -0.5),
        'ffn_w_down': nrm((L, D_FF, d), BETA * D_FF ** -0.5),
        'ln3_g': 1.0 + nrm((L, d), 0.02),
        'ln3_b': nrm((L, d), 0.02),
    }


def reference(x_prompt, x_sample, cache_sb_k, cache_sb_v, cache_fox_k, cache_fox_v, cache_fox_logf,
              cache_mem_k, cache_mem_v, page_table, mem_prompt, w_in, b_forget, gm_ln_g, gm_ln_b,
              gm_w_s, gm_b_s, w_br_gm, w_br_sb, w_br_fox, w_mix_out, ln1_g, ln1_b, mem_w_q, mem_w_k,
              mem_w_v, mem_w_out, ln2_g, ln2_b, ffn_w_gate, ffn_w_up, ffn_w_down, ln3_g, ln3_b):
    B = x_prompt.shape[0]
    xp, xs = x_prompt, x_sample
    sbk_p, sbv_p, fxk_p, fxv_p, fxf_p, mk_p, mv_p = [], [], [], [], [], [], []
    gmv_s, sbk_s, sbv_s, fxk_s, fxv_s, fxf_s = [], [], [], [], [], []
    for l in range(DEPTH):
        gu, gv, sq, sk, sv, fq, fk, fv, logf, gates = _in_proj(xp, w_in[l], b_forget[l])
        y_gm, _ = _gmlp(gu, gv, gm_ln_g[l], gm_ln_b[l], gm_w_s[l], gm_b_s[l])
        y_sb = _sb_prompt(sq, sk, sv)
        y_fx = _fox_prompt(fq, fk, fv, jnp.cumsum(logf, axis=1))
        mix = _merge(gates, y_gm, y_sb, y_fx, w_br_gm[l], w_br_sb[l], w_br_fox[l], w_mix_out[l])
        xp = _layernorm(ALPHA * xp + mix, ln1_g[l], ln1_b[l])
        mk = (mem_prompt @ mem_w_k[l]).reshape(B, -1, MEM_HEADS, MEM_HEAD_DIM)
        mv = (mem_prompt @ mem_w_v[l]).reshape(B, -1, MEM_HEADS, MEM_HEAD_DIM)
        xp = _layernorm(ALPHA * xp + _mem_attend(xp, mk, mv, mem_w_q[l], mem_w_out[l]), ln2_g[l], ln2_b[l])
        xp = _layernorm(ALPHA * xp + _swiglu(xp, ffn_w_gate[l], ffn_w_up[l], ffn_w_down[l]), ln3_g[l], ln3_b[l])
        sbk_p.append(sk); sbv_p.append(sv); fxk_p.append(fk); fxv_p.append(fv); fxf_p.append(logf)
        mk_p.append(mk); mv_p.append(mv)
        gu, gv, sq, sk, sv, fq, fk, fv, logf, gates = _in_proj(xs, w_in[l], b_forget[l])
        y_gm, v_rows = _gmlp(gu, gv, gm_ln_g[l], gm_ln_b[l], gm_w_s[l], gm_b_s[l])
        y_sb = _sb_sample(sq, _gather(cache_sb_k, l, page_table), _gather(cache_sb_v, l, page_table), sk, sv)
        y_fx = _fox_sample(fq, _gather(cache_fox_k, l, page_table), _gather(cache_fox_v, l, page_table), fk, fv,
                           _gather(cache_fox_logf, l, page_table).astype(jnp.float32), logf)
        mix = _merge(gates, y_gm, y_sb, y_fx, w_br_gm[l], w_br_sb[l], w_br_fox[l], w_mix_out[l])
        xs = _layernorm(ALPHA * xs + mix, ln1_g[l], ln1_b[l])
        xs = _layernorm(ALPHA * xs + _mem_attend(xs, cache_mem_k[l], cache_mem_v[l], mem_w_q[l], mem_w_out[l]),
                        ln2_g[l], ln2_b[l])
        xs = _layernorm(ALPHA * xs + _swiglu(xs, ffn_w_gate[l], ffn_w_up[l], ffn_w_down[l]), ln3_g[l], ln3_b[l])
        gmv_s.append(v_rows); sbk_s.append(sk); sbv_s.append(sv)
        fxk_s.append(fk); fxv_s.append(fv); fxf_s.append(logf)
    st = jnp.stack
    return (xp, xs, st(sbk_p), st(sbv_p), st(fxk_p), st(fxv_p), st(fxf_p), st(mk_p), st(mv_p),
            st(gmv_s), st(sbk_s), st(sbv_s), st(fxk_s), st(fxv_s), st(fxf_s))
```

```python
import functools

import jax
import jax.numpy as jnp
from jax import lax
from jax.experimental import pallas as pl
from jax.experimental.pallas import tpu as pltpu

F32 = jnp.float32
BF16 = jnp.bfloat16

D_MODEL = 2048
DEPTH = 2
HEAD_DIM = 128
N_HEADS = 8
BRANCH_W = N_HEADS * HEAD_DIM
GM_CHUNK = 128
MEM_TOKENS = 256
MEM_HEADS = 4
MEM_W = MEM_HEADS * HEAD_DIM
PAGE_SIZE = 128
ALPHA = (2.0 * DEPTH) ** 0.25
LN_EPS = 1e-5
ATTN_SCALE = HEAD_DIM ** -0.5

LANES = 128
SUBLANES = 8
VMEM_CAP = 56 << 20
PAGES_PER_STEP = 8
T_PAD = SUBLANES


def _vmem(nbytes):
    return int(min(VMEM_CAP, max(16 << 20, 2 * nbytes + (8 << 20))))


def _nbytes(shape, dtype):
    n = 1
    for s in shape:
        n *= s
    return n * jnp.dtype(dtype).itemsize


def _sigmoid(x):
    return 1.0 / (1.0 + jnp.exp(-x))


def _log_sigmoid(x):
    return jnp.minimum(x, 0.0) - jnp.log(1.0 + jnp.exp(-jnp.abs(x)))


def _layernorm(y, g, b):
    mu = jnp.mean(y, axis=-1, keepdims=True)
    d = y - mu
    var = jnp.mean(d * d, axis=-1, keepdims=True)
    return d * lax.rsqrt(var + LN_EPS) * g + b


def _split_dot(x, m_bf16):
    hi = x.astype(BF16)
    lo = (x - hi.astype(F32)).astype(BF16)
    return (jnp.dot(hi, m_bf16, preferred_element_type=F32)
            + jnp.dot(lo, m_bf16, preferred_element_type=F32))


def _mm_kernel(*refs, n_pairs, n_extra, n_out, epilogue, nk):
    pairs = refs[:2 * n_pairs]
    extra = refs[2 * n_pairs:2 * n_pairs + n_extra]
    outs = refs[2 * n_pairs + n_extra:2 * n_pairs + n_extra + n_out]

    def finish(accs):
        res = epilogue(*accs, *[e[...] for e in extra])
        for o, r in zip(outs, res):
            o[...] = r.astype(o.dtype)

    parts = [jnp.dot(pairs[2 * p][...], pairs[2 * p + 1][...], preferred_element_type=F32)
             for p in range(n_pairs)]
    if nk == 1:
        finish(parts)
    else:
        acc_ref = refs[-1]
        k = pl.program_id(2)

        @pl.when(k == 0)
        def _():
            acc_ref[...] = parts[0]

        @pl.when(k > 0)
        def _():
            acc_ref[...] += parts[0]

        @pl.when(k == nk - 1)
        def _():
            finish([acc_ref[...]])


def _mm(pairs, epilogue, out_dtypes, *, tm, tn, tk=None, extras=(), n_cols=None):
    m = pairs[0][0].shape[0]
    n = n_cols if n_cols is not None else pairs[0][1].shape[1]
    tm = min(tm, m)
    tn = min(tn, n)
    assert m % tm == 0 and n % tn == 0
    kdim = pairs[0][0].shape[1]
    nk = 1 if tk is None else kdim // tk
    assert nk == 1 or (len(pairs) == 1 and kdim % tk == 0)
    in_specs, args, nbytes = [], [], 0
    for x, w in pairs:
        kk = x.shape[1] if nk == 1 else tk
        in_specs.append(pl.BlockSpec((tm, kk), lambda i, j, k: (i, k)))
        in_specs.append(pl.BlockSpec((kk, tn), lambda i, j, k: (k, j)))
        args += [x, w]
        nbytes += _nbytes((tm, kk), x.dtype) + _nbytes((kk, tn), w.dtype)
    for e in extras:
        if e[0] == 'row':
            in_specs.append(pl.BlockSpec((1, tn), lambda i, j, k: (0, j)))
            nbytes += _nbytes((SUBLANES, tn), e[1].dtype)
        else:
            off = e[2]
            in_specs.append(pl.BlockSpec((tm, tn), lambda i, j, k, off=off: (i, j + off)))
            nbytes += _nbytes((tm, tn), e[1].dtype)
        args.append(e[1])
    out_shape = [jax.ShapeDtypeStruct((m, n), dt) for dt in out_dtypes]
    out_specs = [pl.BlockSpec((tm, tn), lambda i, j, k: (i, j)) for _ in out_dtypes]
    nbytes += sum(_nbytes((tm, tn), dt) for dt in out_dtypes)
    nbytes += (len(pairs) + 2) * _nbytes((tm, tn), F32)
    scratch = [pltpu.VMEM((tm, tn), F32)] if nk > 1 else []
    kern = functools.partial(_mm_kernel, n_pairs=len(pairs), n_extra=len(extras),
                             n_out=len(out_dtypes), epilogue=epilogue, nk=nk)
    return pl.pallas_call(
        kern,
        grid=(m // tm, n // tn, nk),
        in_specs=in_specs,
        out_specs=out_specs,
        out_shape=out_shape,
        scratch_shapes=scratch,
        compiler_params=pltpu.CompilerParams(
            dimension_semantics=("parallel", "parallel", "arbitrary"),
            vmem_limit_bytes=_vmem(nbytes)),
    )(*args)


def _epi_gelu(acc):
    return (jax.nn.gelu(acc),)


def _epi_gelu_ln(acc, g, b):
    return (_layernorm(jax.nn.gelu(acc), g, b),)


def _epi_scale(acc):
    return (acc * ATTN_SCALE,)


def _epi_dual(acc):
    return (acc, acc)


def _epi_logsig(acc, b):
    return (_log_sigmoid(acc + b),)


def _epi_sigmoid(acc):
    return (_sigmoid(acc),)


def _epi_merge(a_gm, a_sb, a_fx, g_gm, g_sb, g_fx):
    return (g_gm.astype(F32) * a_gm + g_sb.astype(F32) * a_sb + g_fx.astype(F32) * a_fx,)


def _epi_resid_ln(acc, resid, g, b):
    y = _layernorm(ALPHA * resid + acc, g, b)
    return (y, y)


def _epi_swiglu(a_gate, a_up):
    return (a_gate * _sigmoid(a_gate) * a_up,)


def _gmlp_kernel(u_ref, v_ref, ws_ref, bs_ref, o_ref, *, n_chunks, rows_per_request):
    r = lax.broadcasted_iota(jnp.int32, (GM_CHUNK, GM_CHUNK), 0)
    c = lax.broadcasted_iota(jnp.int32, (GM_CHUNK, GM_CHUNK), 1)
    mask = c <= r
    if rows_per_request is not None:
        mask = jnp.logical_and(mask, (r // rows_per_request) == (c // rows_per_request))
    for g in range(N_HEADS):
        w = jnp.where(mask, ws_ref[g], 0.0).astype(BF16)
        b = bs_ref[g]
        cols = slice(g * LANES, (g + 1) * LANES)
        for n in range(n_chunks):
            rows = slice(n * GM_CHUNK, (n + 1) * GM_CHUNK)
            mixed = jnp.dot(w, v_ref[rows, cols].astype(BF16), preferred_element_type=F32) + b
            o_ref[rows, cols] = (u_ref[rows, cols].astype(F32) * mixed).astype(o_ref.dtype)


def _gmlp(u, v, ws, bs, *, tm, rows_per_request=None):
    m = u.shape[0]
    tm = min(tm, m)
    nbytes = _nbytes((tm, BRANCH_W), BF16) * 2 + _nbytes((tm, BRANCH_W), F32)
    kern = functools.partial(_gmlp_kernel, n_chunks=tm // GM_CHUNK, rows_per_request=rows_per_request)
    return pl.pallas_call(
        kern,
        grid=(m // tm,),
        in_specs=[pl.BlockSpec((tm, BRANCH_W), lambda i: (i, 0)),
                  pl.BlockSpec((tm, BRANCH_W), lambda i: (i, 0)),
                  pl.BlockSpec((N_HEADS, GM_CHUNK, GM_CHUNK), lambda i: (0, 0, 0)),
                  pl.BlockSpec((N_HEADS, GM_CHUNK, 1), lambda i: (0, 0, 0))],
        out_specs=pl.BlockSpec((tm, BRANCH_W), lambda i: (i, 0)),
        out_shape=jax.ShapeDtypeStruct((m, BRANCH_W), BF16),
        compiler_params=pltpu.CompilerParams(dimension_semantics=("parallel",),
                                             vmem_limit_bytes=_vmem(nbytes)),
    )(u, v, ws, bs)


def _suffix_matrix(n):
    r = lax.broadcasted_iota(jnp.int32, (n, n), 0)
    c = lax.broadcasted_iota(jnp.int32, (n, n), 1)
    return jnp.where(r > c, 1.0, 0.0).astype(BF16)


def _sb_prompt_kernel(q_ref, k_ref, v_ref, o_ref, *, tq):
    i = pl.program_id(1)
    q = q_ref[...]
    u = _suffix_matrix(tq)
    row = lax.broadcasted_iota(jnp.int32, (tq, tq), 0)
    col = lax.broadcasted_iota(jnp.int32, (tq, tq), 1)
    visible = col < row

    def block(j, carry, acc, masked):
        start = pl.multiple_of(j * tq, tq)
        kb = k_ref[pl.ds(start, tq), :]
        vb = v_ref[pl.ds(start, tq), :]
        z = lax.dot_general(q, kb, (((1,), (1,)), ((), ())), preferred_element_type=F32)
        ls = _log_sigmoid(z)
        lk = ls - z
        if masked:
            lk = jnp.where(visible, lk, 0.0)
        p = jnp.exp(ls + _split_dot(lk, u) + carry)
        if masked:
            p = jnp.where(visible, p, 0.0)
        acc = acc + jnp.dot(p.astype(BF16), vb, preferred_element_type=F32)
        carry = carry + jnp.sum(lk, axis=1, keepdims=True)
        return carry, acc

    carry, acc = block(i, jnp.zeros((tq, 1), F32), jnp.zeros((tq, HEAD_DIM), F32), True)

    def body(n, state):
        return block(i - 1 - n, state[0], state[1], False)

    carry, acc = lax.fori_loop(0, i, body, (carry, acc))
    o_ref[...] = acc.astype(o_ref.dtype)


def _sb_prompt(qq, kv, *, q_off, k_off, v_off, tq):
    s = qq.shape[0]
    nbytes = 2 * _nbytes((s, HEAD_DIM), BF16) + 8 * _nbytes((tq, tq), F32)
    return pl.pallas_call(
        functools.partial(_sb_prompt_kernel, tq=tq),
        grid=(N_HEADS, s // tq),
        in_specs=[pl.BlockSpec((tq, HEAD_DIM), lambda h, i: (i, q_off + h)),
                  pl.BlockSpec((s, HEAD_DIM), lambda h, i: (0, k_off + h)),
                  pl.BlockSpec((s, HEAD_DIM), lambda h, i: (0, v_off + h))],
        out_specs=pl.BlockSpec((tq, HEAD_DIM), lambda h, i: (i, h)),
        out_shape=jax.ShapeDtypeStruct((s, BRANCH_W), BF16),
        compiler_params=pltpu.CompilerParams(dimension_semantics=("parallel", "arbitrary"),
                                             vmem_limit_bytes=_vmem(nbytes)),
    )(qq, kv, kv)


def _neg_cumsum_kernel(lf_ref, o_ref, carry_ref):
    @pl.when(pl.program_id(0) == 0)
    def _():
        carry_ref[...] = jnp.zeros_like(carry_ref)

    n = lf_ref.shape[0]
    r = lax.broadcasted_iota(jnp.int32, (n, n), 0)
    c = lax.broadcasted_iota(jnp.int32, (n, n), 1)
    incl = jnp.where(r <= c, 1.0, 0.0).astype(F32)
    lft = lf_ref[...].T
    cum = jnp.dot(lft, incl, preferred_element_type=F32, precision=lax.Precision.HIGHEST)
    cum = cum + carry_ref[...]
    carry_ref[...] = cum[:, n - 1:n]
    o_ref[...] = -cum[:N_HEADS, :]


def _neg_cumsum(logf_pad):
    s = logf_pad.shape[0]
    n = LANES
    return pl.pallas_call(
        _neg_cumsum_kernel,
        grid=(s // n,),
        in_specs=[pl.BlockSpec((n, LANES), lambda i: (i, 0))],
        out_specs=pl.BlockSpec((N_HEADS, n), lambda i: (0, i)),
        out_shape=jax.ShapeDtypeStruct((N_HEADS, s), F32),
        scratch_shapes=[pltpu.VMEM((LANES, 1), F32)],
        compiler_params=pltpu.CompilerParams(dimension_semantics=("arbitrary",)),
    )(logf_pad)


def _fox_prompt_kernel(q_ref, k_ref, v_ref, nc_ref, o_ref, *, tq):
    i = pl.program_id(1)
    q = q_ref[...]
    row = lax.broadcasted_iota(jnp.int32, (tq, tq), 0)
    col = lax.broadcasted_iota(jnp.int32, (tq, tq), 1)
    visible = col <= row

    def block(j, m, l, acc, masked):
        start = pl.multiple_of(j * tq, tq)
        kb = k_ref[pl.ds(start, tq), :]
        vb = v_ref[pl.ds(start, tq), :]
        z = lax.dot_general(q, kb, (((1,), (1,)), ((), ())), preferred_element_type=F32)
        logits = z + nc_ref[:, pl.ds(start, tq)]
        if masked:
            logits = jnp.where(visible, logits, -jnp.inf)
        m_new = jnp.maximum(m, jnp.max(logits, axis=1, keepdims=True))
        alpha = jnp.exp(m - m_new)
        p = jnp.exp(logits - m_new)
        l = alpha * l + jnp.sum(p, axis=1, keepdims=True)
        acc = alpha * acc + jnp.dot(p.astype(BF16), vb, preferred_element_type=F32)
        return m_new, l, acc

    state = block(i, jnp.full((tq, 1), -jnp.inf, F32), jnp.zeros((tq, 1), F32),
                  jnp.zeros((tq, HEAD_DIM), F32), True)

    def body(n, st):
        return block(n, st[0], st[1], st[2], False)

    m, l, acc = lax.fori_loop(0, i, body, state)
    o_ref[...] = (acc / l).astype(o_ref.dtype)


def _fox_prompt(qq, kv, neg_c, *, q_off, k_off, v_off, tq):
    s = qq.shape[0]
    nbytes = 2 * _nbytes((s, HEAD_DIM), BF16) + 8 * _nbytes((tq, tq), F32)
    return pl.pallas_call(
        functools.partial(_fox_prompt_kernel, tq=tq),
        grid=(N_HEADS, s // tq),
        in_specs=[pl.BlockSpec((tq, HEAD_DIM), lambda h, i: (i, q_off + h)),
                  pl.BlockSpec((s, HEAD_DIM), lambda h, i: (0, k_off + h)),
                  pl.BlockSpec((s, HEAD_DIM), lambda h, i: (0, v_off + h)),
                  pl.BlockSpec((None, 1, s), lambda h, i: (h, 0, 0))],
        out_specs=pl.BlockSpec((tq, HEAD_DIM), lambda h, i: (i, h)),
        out_shape=jax.ShapeDtypeStruct((s, BRANCH_W), BF16),
        compiler_params=pltpu.CompilerParams(dimension_semantics=("parallel", "arbitrary"),
                                             vmem_limit_bytes=_vmem(nbytes)),
    )(qq, kv, kv, neg_c.reshape(N_HEADS, 1, s))


def _page_heads(ref):
    return jnp.stack([ref[pl.ds(h, PAGE_SIZE, stride=N_HEADS), :] for h in range(N_HEADS)]).astype(BF16)


def _scores(q, kb):
    return jnp.einsum('htd,hsd->hts', q, kb, preferred_element_type=F32)


def _weighted(p, vb):
    return jnp.einsum('hts,hsd->htd', p.astype(BF16), vb, preferred_element_type=F32)


def _new_token_mask(inclusive):
    t = lax.broadcasted_iota(jnp.int32, (N_HEADS, T_PAD, PAGE_SIZE), 1)
    s = lax.broadcasted_iota(jnp.int32, (N_HEADS, T_PAD, PAGE_SIZE), 2)
    return (s <= t) if inclusive else (s < t)


def _sb_sample_kernel(pt_ref, q_ref, kn_ref, vn_ref, *rest, n_new):
    kp = rest[:PAGES_PER_STEP]
    vp = rest[PAGES_PER_STEP:2 * PAGES_PER_STEP]
    o_ref, carry_ref, acc_ref = rest[2 * PAGES_PER_STEP:]
    j = pl.program_id(1)
    q = q_ref[...]
    u = _suffix_matrix(PAGE_SIZE)

    def block(kb, vb, carry, acc, mask):
        z = _scores(q, kb)
        ls = _log_sigmoid(z)
        lk = ls - z
        if mask is not None:
            lk = jnp.where(mask, lk, 0.0)
        after = _split_dot(lk.reshape(N_HEADS * T_PAD, PAGE_SIZE), u).reshape(lk.shape)
        p = jnp.exp(ls + after + carry)
        if mask is not None:
            p = jnp.where(mask, p, 0.0)
        return carry + jnp.sum(lk, axis=-1, keepdims=True), acc + _weighted(p, vb)

    @pl.when(j == 0)
    def _():
        s = lax.broadcasted_iota(jnp.int32, (N_HEADS, T_PAD, PAGE_SIZE), 2)
        mask = jnp.logical_and(_new_token_mask(False), s < n_new)
        carry, acc = block(kn_ref[...], vn_ref[...], jnp.zeros((N_HEADS, T_PAD, 1), F32),
                           jnp.zeros((N_HEADS, T_PAD, HEAD_DIM), F32), mask)
        carry_ref[...] = carry
        acc_ref[...] = acc

    carry, acc = carry_ref[...], acc_ref[...]
    for g in range(PAGES_PER_STEP):
        carry, acc = block(_page_heads(kp[g]), _page_heads(vp[g]), carry, acc, None)
    carry_ref[...] = carry
    acc_ref[...] = acc

    @pl.when(j == pl.num_programs(1) - 1)
    def _():
        o_ref[...] = acc.astype(o_ref.dtype)


def _fox_sample_kernel(pt_ref, q_ref, kn_ref, vn_ref, lfn_ref, *rest, n_new):
    kp = rest[:PAGES_PER_STEP]
    vp = rest[PAGES_PER_STEP:2 * PAGES_PER_STEP]
    lfp = rest[2 * PAGES_PER_STEP:3 * PAGES_PER_STEP]
    o_ref, carry_ref, m_ref, l_ref, acc_ref = rest[3 * PAGES_PER_STEP:]
    j = pl.program_id(1)
    q = q_ref[...]
    u = _suffix_matrix(PAGE_SIZE)

    def block(kb, vb, lf, carry, m, l, acc, mask):
        suffix = _split_dot(lf, u)
        logits = _scores(q, kb) + (carry + suffix[:, None, :])
        if mask is not None:
            logits = jnp.where(mask, logits, -jnp.inf)
        m_new = jnp.maximum(m, jnp.max(logits, axis=-1, keepdims=True))
        alpha = jnp.exp(m - m_new)
        p = jnp.exp(logits - m_new)
        l = alpha * l + jnp.sum(p, axis=-1, keepdims=True)
        acc = alpha * acc + _weighted(p, vb)
        carry = carry + jnp.sum(lf, axis=-1, keepdims=True)[:, None, :]
        return carry, m_new, l, acc

    @pl.when(j == 0)
    def _():
        t = lax.broadcasted_iota(jnp.int32, (N_HEADS, T_PAD, PAGE_SIZE), 1)
        s = lax.broadcasted_iota(jnp.int32, (N_HEADS, T_PAD, PAGE_SIZE), 2)
        mask = jnp.logical_and(_new_token_mask(True), s < n_new)
        lfn = lfn_ref[...]
        suffix_new = _split_dot(lfn, u)
        carry0 = -jnp.sum(jnp.where(s == t, suffix_new[:, None, :], 0.0), axis=-1, keepdims=True)
        carry, m, l, acc = block(kn_ref[...], vn_ref[...], lfn, carry0,
                                 jnp.full((N_HEADS, T_PAD, 1), -jnp.inf, F32),
                                 jnp.zeros((N_HEADS, T_PAD, 1), F32),
                                 jnp.zeros((N_HEADS, T_PAD, HEAD_DIM), F32), mask)
        carry_ref[...] = carry
        m_ref[...] = m
        l_ref[...] = l
        acc_ref[...] = acc

    carry, m, l, acc = carry_ref[...], m_ref[...], l_ref[...], acc_ref[...]
    for g in range(PAGES_PER_STEP):
        carry, m, l, acc = block(_page_heads(kp[g]), _page_heads(vp[g]), lfp[g][...],
                                 carry, m, l, acc, None)
    carry_ref[...] = carry
    m_ref[...] = m
    l_ref[...] = l
    acc_ref[...] = acc

    @pl.when(j == pl.num_programs(1) - 1)
    def _():
        o_ref[...] = (acc / l).astype(o_ref.dtype)


def _sample_attention(page_table, q, k_new, v_new, k_cache, v_cache, layer, *, n_new,
                      lf_new=None, lf_cache=None):
    n_req, n_pages = page_table.shape
    steps = n_pages // PAGES_PER_STEP
    fox = lf_cache is not None

    def page_map(g):
        def index(b, j, pt):
            return (layer, pt[b, n_pages - 1 - (j * PAGES_PER_STEP + g)], 0, 0)
        return index

    small = lambda b, j, pt: (b, 0, 0, 0)
    in_specs = [pl.BlockSpec((None, N_HEADS, T_PAD, HEAD_DIM), small),
                pl.BlockSpec((None, N_HEADS, PAGE_SIZE, HEAD_DIM), small),
                pl.BlockSpec((None, N_HEADS, PAGE_SIZE, HEAD_DIM), small)]
    args = [q, k_new, v_new]
    if fox:
        in_specs.append(pl.BlockSpec((None, N_HEADS, PAGE_SIZE), lambda b, j, pt: (b, 0, 0)))
        args.append(lf_new)
    page_rows = PAGE_SIZE * N_HEADS
    for cache in (k_cache, v_cache):
        for g in range(PAGES_PER_STEP):
            in_specs.append(pl.BlockSpec((None, None, page_rows, HEAD_DIM), page_map(g)))
            args.append(cache)
    if fox:
        for g in range(PAGES_PER_STEP):
            in_specs.append(pl.BlockSpec((None, None, N_HEADS, PAGE_SIZE), page_map(g)))
            args.append(lf_cache)
    state = (N_HEADS, T_PAD, 1)
    scratch = [pltpu.VMEM(state, F32)] * (3 if fox else 1) + [pltpu.VMEM((N_HEADS, T_PAD, HEAD_DIM), F32)]
    nbytes = 2 * PAGES_PER_STEP * _nbytes((page_rows, HEAD_DIM), F32) * 2
    kern = functools.partial(_fox_sample_kernel if fox else _sb_sample_kernel, n_new=n_new)
    return pl.pallas_call(
        kern,
        grid_spec=pltpu.PrefetchScalarGridSpec(
            num_scalar_prefetch=1,
            grid=(n_req, steps),
            in_specs=in_specs,
            out_specs=pl.BlockSpec((None, N_HEADS, T_PAD, HEAD_DIM), small),
            scratch_shapes=scratch),
        out_shape=jax.ShapeDtypeStruct((n_req, N_HEADS, T_PAD, HEAD_DIM), BF16),
        compiler_params=pltpu.CompilerParams(dimension_semantics=("parallel", "arbitrary"),
                                             vmem_limit_bytes=_vmem(nbytes)),
    )(page_table, *args)


def _softmax_rows(s):
    m = jnp.max(s, axis=-1, keepdims=True)
    p = jnp.exp(s - m)
    return p, jnp.sum(p, axis=-1, keepdims=True)


def _mem_prompt_kernel(q_ref, k_ref, v_ref, o_ref):
    for h in range(MEM_HEADS):
        cols = slice(h * HEAD_DIM, (h + 1) * HEAD_DIM)
        s = lax.dot_general(q_ref[:, cols], k_ref[:, cols], (((1,), (1,)), ((), ())),
                            preferred_element_type=F32)
        p, l = _softmax_rows(s)
        o = jnp.dot(p.astype(BF16), v_ref[:, cols], preferred_element_type=F32)
        o_ref[:, cols] = (o / l).astype(o_ref.dtype)


def _mem_prompt(q, mk, mv, *, tm):
    m = q.shape[0]
    nbytes = 2 * _nbytes((tm, MEM_W), BF16) + 4 * _nbytes((tm, MEM_TOKENS), F32)
    return pl.pallas_call(
        _mem_prompt_kernel,
        grid=(m // tm,),
        in_specs=[pl.BlockSpec((tm, MEM_W), lambda i: (i, 0)),
                  pl.BlockSpec((MEM_TOKENS, MEM_W), lambda i: (0, 0)),
                  pl.BlockSpec((MEM_TOKENS, MEM_W), lambda i: (0, 0))],
        out_specs=pl.BlockSpec((tm, MEM_W), lambda i: (i, 0)),
        out_shape=jax.ShapeDtypeStruct((m, MEM_W), BF16),
        compiler_params=pltpu.CompilerParams(dimension_semantics=("parallel",),
                                             vmem_limit_bytes=_vmem(nbytes)),
    )(q, mk, mv)


def _mem_sample_kernel(q_ref, k_ref, v_ref, o_ref):
    for h in range(MEM_HEADS):
        cols = slice(h * HEAD_DIM, (h + 1) * HEAD_DIM)
        kh = k_ref[pl.ds(h, MEM_TOKENS, stride=MEM_HEADS), :].astype(BF16)
        vh = v_ref[pl.ds(h, MEM_TOKENS, stride=MEM_HEADS), :].astype(BF16)
        s = lax.dot_general(q_ref[:, cols], kh, (((1,), (1,)), ((), ())), preferred_element_type=F32)
        p, l = _softmax_rows(s)
        o = jnp.dot(p.astype(BF16), vh, preferred_element_type=F32)
        o_ref[:, cols] = (o / l).astype(o_ref.dtype)


def _mem_sample(q, k_cache, v_cache, layer):
    n_req = q.shape[0]
    rows = MEM_TOKENS * MEM_HEADS
    cache_spec = pl.BlockSpec((None, None, rows, HEAD_DIM), lambda b: (layer, b, 0, 0))
    return pl.pallas_call(
        _mem_sample_kernel,
        grid=(n_req,),
        in_specs=[pl.BlockSpec((None, T_PAD, MEM_W), lambda b: (b, 0, 0)), cache_spec, cache_spec],
        out_specs=pl.BlockSpec((None, T_PAD, MEM_W), lambda b: (b, 0, 0)),
        out_shape=jax.ShapeDtypeStruct((n_req, T_PAD, MEM_W), BF16),
        compiler_params=pltpu.CompilerParams(dimension_semantics=("parallel",)),
    )(q, k_cache, v_cache)


def _layer_weights(l, w_in, b_forget, gm_ln_g, gm_ln_b, gm_w_s, gm_b_s, w_br_gm, w_br_sb, w_br_fox,
                   w_mix_out, ln1_g, ln1_b, mem_w_q, mem_w_k, mem_w_v, mem_w_out, ln2_g, ln2_b,
                   ffn_w_gate, ffn_w_up, ffn_w_down, ln3_g, ln3_b):
    w = w_in[l]
    bw = BRANCH_W
    o_fg = 8 * bw
    o_gate = o_fg + N_HEADS
    row = lambda a: a[l].reshape(1, -1).astype(F32)
    pad_fg = jnp.zeros((D_MODEL, LANES - N_HEADS), F32)
    return dict(
        w_u=w[:, 0:bw].astype(BF16),
        w_v=w[:, bw:2 * bw].astype(BF16),
        w_q=jnp.concatenate([w[:, 2 * bw:3 * bw], w[:, 5 * bw:6 * bw]], axis=1).astype(BF16),
        w_kv=jnp.concatenate([w[:, 3 * bw:5 * bw], w[:, 6 * bw:8 * bw]], axis=1).astype(BF16),
        w_fg=jnp.concatenate([w[:, o_fg:o_gate], pad_fg], axis=1).astype(BF16),
        b_fg=jnp.concatenate([b_forget[l].astype(F32), jnp.zeros((LANES - N_HEADS,), F32)]).reshape(1, LANES),
        w_gate=w[:, o_gate:].astype(BF16),
        gm_ln_g=row(gm_ln_g), gm_ln_b=row(gm_ln_b),
        gm_w_s=gm_w_s[l].astype(F32), gm_b_s=gm_b_s[l].astype(F32),
        w_br_gm=w_br_gm[l].astype(BF16), w_br_sb=w_br_sb[l].astype(BF16),
        w_br_fox=w_br_fox[l].astype(BF16), w_mix_out=w_mix_out[l].astype(BF16),
        ln1_g=row(ln1_g), ln1_b=row(ln1_b),
        mem_w_q=mem_w_q[l].astype(BF16), mem_w_k=mem_w_k[l].astype(BF16),
        mem_w_v=mem_w_v[l].astype(BF16), mem_w_out=mem_w_out[l].astype(BF16),
        ln2_g=row(ln2_g), ln2_b=row(ln2_b),
        ffn_w_gate=ffn_w_gate[l].astype(BF16), ffn_w_up=ffn_w_up[l].astype(BF16),
        ffn_w_down=ffn_w_down[l].astype(BF16),
        ln3_g=row(ln3_g), ln3_b=row(ln3_b),
    )


def _in_proj(x_bf, p, *, tm):
    u, = _mm([(x_bf, p['w_u'])], _epi_gelu, [BF16], tm=tm, tn=512)
    v, = _mm([(x_bf, p['w_v'])], _epi_gelu_ln, [F32], tm=min(tm, 512), tn=BRANCH_W,
             extras=[('row', p['gm_ln_g']), ('row', p['gm_ln_b'])])
    qq, = _mm([(x_bf, p['w_q'])], _epi_scale, [BF16], tm=tm, tn=512)
    kv_f32, kv_bf = _mm([(x_bf, p['w_kv'])], _epi_dual, [F32, BF16], tm=tm, tn=512)
    logf, = _mm([(x_bf, p['w_fg'])], _epi_logsig, [F32], tm=tm, tn=LANES, extras=[('row', p['b_fg'])])
    gates, = _mm([(x_bf, p['w_gate'])], _epi_sigmoid, [BF16], tm=tm, tn=512)
    return u, v, qq, kv_f32, kv_bf, logf, gates


def _post_mixers(x, y_gm, y_sb, y_fx, gates, p, *, tm):
    gate_blocks = D_MODEL // 512
    h, = _mm([(y_gm, p['w_br_gm']), (y_sb, p['w_br_sb']), (y_fx, p['w_br_fox'])], _epi_merge, [BF16],
             tm=min(tm, 512), tn=512,
             extras=[('tile', gates, 0), ('tile', gates, gate_blocks), ('tile', gates, 2 * gate_blocks)])
    return _mm([(h, p['w_mix_out'])], _epi_resid_ln, [F32, BF16], tm=min(tm, 256), tn=D_MODEL,
               extras=[('tile', x, 0), ('row', p['ln1_g']), ('row', p['ln1_b'])])


def _mem_query(x1_bf, p, *, tm):
    q, = _mm([(x1_bf, p['mem_w_q'])], _epi_scale, [BF16], tm=tm, tn=MEM_W)
    return q


def _post_mem(x1, o, p, *, tm):
    return _mm([(o, p['mem_w_out'])], _epi_resid_ln, [F32, BF16], tm=min(tm, 256), tn=D_MODEL,
               extras=[('tile', x1, 0), ('row', p['ln2_g']), ('row', p['ln2_b'])])


def _ffn(x2, x2_bf, p, *, tm):
    d_ff = p['ffn_w_gate'].shape[1]
    hid, = _mm([(x2_bf, p['ffn_w_gate']), (x2_bf, p['ffn_w_up'])], _epi_swiglu, [BF16], tm=tm, tn=512)
    return _mm([(hid, p['ffn_w_down'])], _epi_resid_ln, [F32, BF16], tm=min(tm, 512), tn=D_MODEL,
               tk=d_ff // 4,
               extras=[('tile', x2, 0), ('row', p['ln3_g']), ('row', p['ln3_b'])])


def _heads_first(a, n_req, t, pad_to, dtype):
    a = a.reshape(n_req, t, N_HEADS, HEAD_DIM).transpose(0, 2, 1, 3).astype(dtype)
    return jnp.pad(a, ((0, 0), (0, 0), (0, pad_to - t), (0, 0)))


def kernel(x_prompt, x_sample, cache_sb_k, cache_sb_v, cache_fox_k, cache_fox_v, cache_fox_logf,
           cache_mem_k, cache_mem_v, page_table, mem_prompt, w_in, b_forget, gm_ln_g, gm_ln_b,
           gm_w_s, gm_b_s, w_br_gm, w_br_sb, w_br_fox, w_mix_out, ln1_g, ln1_b, mem_w_q, mem_w_k,
           mem_w_v, mem_w_out, ln2_g, ln2_b, ffn_w_gate, ffn_w_up, ffn_w_down, ln3_g, ln3_b):
    batch, seq, _ = x_prompt.shape
    n_req, t_new, _ = x_sample.shape
    assert batch == 1 and t_new <= T_PAD
    n_pool = cache_sb_k.shape[1]
    rows_s = n_req * t_new
    bw = BRANCH_W

    page_rows = PAGE_SIZE * N_HEADS
    pool_view = lambda c: c.reshape(DEPTH, n_pool, page_rows, HEAD_DIM)
    c_sb_k, c_sb_v, c_fx_k, c_fx_v = map(pool_view, (cache_sb_k, cache_sb_v, cache_fox_k, cache_fox_v))
    c_fx_lf = cache_fox_logf.astype(F32).transpose(0, 1, 3, 2)
    mem_view = lambda c: c.reshape(DEPTH, n_req, MEM_TOKENS * MEM_HEADS, HEAD_DIM)
    c_mem_k, c_mem_v = mem_view(cache_mem_k), mem_view(cache_mem_v)
    mem_bf = mem_prompt.reshape(MEM_TOKENS, D_MODEL).astype(BF16)

    xp = x_prompt.reshape(seq, D_MODEL)
    xs = x_sample.reshape(rows_s, D_MODEL)
    xp_bf, xs_bf = xp.astype(BF16), xs.astype(BF16)
    outs = {k: [] for k in ('sbk_p', 'sbv_p', 'fxk_p', 'fxv_p', 'fxf_p', 'mk_p', 'mv_p',
                            'gmv_s', 'sbk_s', 'sbv_s', 'fxk_s', 'fxv_s', 'fxf_s')}
    tm_p, tm_s, tq = 1024, rows_s, 256
    for l in range(DEPTH):
        p = _layer_weights(l, w_in, b_forget, gm_ln_g, gm_ln_b, gm_w_s, gm_b_s, w_br_gm, w_br_sb,
                           w_br_fox, w_mix_out, ln1_g, ln1_b, mem_w_q, mem_w_k, mem_w_v, mem_w_out,
                           ln2_g, ln2_b, ffn_w_gate, ffn_w_up, ffn_w_down, ln3_g, ln3_b)

        u, v, qq, kv, kv_bf, logf, gates = _in_proj(xp_bf, p, tm=tm_p)
        y_gm = _gmlp(u, v, p['gm_w_s'], p['gm_b_s'].reshape(N_HEADS, GM_CHUNK, 1), tm=tm_p)
        y_sb = _sb_prompt(qq, kv_bf, q_off=0, k_off=0, v_off=N_HEADS, tq=tq)
        y_fx = _fox_prompt(qq, kv_bf, _neg_cumsum(logf), q_off=N_HEADS, k_off=2 * N_HEADS,
                           v_off=3 * N_HEADS, tq=tq)
        x1, x1_bf = _post_mixers(xp, y_gm, y_sb, y_fx, gates, p, tm=tm_p)
        mk, mk_bf = _mm([(mem_bf, p['mem_w_k'])], _epi_dual, [F32, BF16], tm=MEM_TOKENS, tn=MEM_W)
        mv, mv_bf = _mm([(mem_bf, p['mem_w_v'])], _epi_dual, [F32, BF16], tm=MEM_TOKENS, tn=MEM_W)
        o_mem = _mem_prompt(_mem_query(x1_bf, p, tm=tm_p), mk_bf, mv_bf, tm=512)
        x2, x2_bf = _post_mem(x1, o_mem, p, tm=tm_p)
        xp, xp_bf = _ffn(x2, x2_bf, p, tm=tm_p)
        heads = lambda a: a.reshape(batch, -1, N_HEADS, HEAD_DIM)
        outs['sbk_p'].append(heads(kv[:, 0:bw]))
        outs['sbv_p'].append(heads(kv[:, bw:2 * bw]))
        outs['fxk_p'].append(heads(kv[:, 2 * bw:3 * bw]))
        outs['fxv_p'].append(heads(kv[:, 3 * bw:4 * bw]))
        outs['fxf_p'].append(logf[:, :N_HEADS].reshape(batch, seq, N_HEADS))
        outs['mk_p'].append(mk.reshape(batch, MEM_TOKENS, MEM_HEADS, HEAD_DIM))
        outs['mv_p'].append(mv.reshape(batch, MEM_TOKENS, MEM_HEADS, HEAD_DIM))

        u, v, qq, kv, kv_bf, logf, gates = _in_proj(xs_bf, p, tm=tm_s)
        ws_s = jnp.tile(p['gm_w_s'][:, :t_new, :t_new], (1, GM_CHUNK // t_new, GM_CHUNK // t_new))
        bs_s = jnp.tile(p['gm_b_s'][:, :t_new], (1, GM_CHUNK // t_new)).reshape(N_HEADS, GM_CHUNK, 1)
        y_gm = _gmlp(u, v, ws_s, bs_s, tm=tm_s, rows_per_request=t_new)
        hf = functools.partial(_heads_first, n_req=n_req, t=t_new, dtype=BF16)
        q_sb, q_fx = hf(qq[:, :bw], pad_to=T_PAD), hf(qq[:, bw:], pad_to=T_PAD)
        kn_sb, vn_sb = hf(kv_bf[:, 0:bw], pad_to=PAGE_SIZE), hf(kv_bf[:, bw:2 * bw], pad_to=PAGE_SIZE)
        kn_fx, vn_fx = hf(kv_bf[:, 2 * bw:3 * bw], pad_to=PAGE_SIZE), hf(kv_bf[:, 3 * bw:], pad_to=PAGE_SIZE)
        lf_s = logf[:, :N_HEADS].reshape(n_req, t_new, N_HEADS)
        lf_new = jnp.pad(lf_s.transpose(0, 2, 1), ((0, 0), (0, 0), (0, PAGE_SIZE - t_new)))
        o_sb = _sample_attention(page_table, q_sb, kn_sb, vn_sb, c_sb_k, c_sb_v, l, n_new=t_new)
        o_fx = _sample_attention(page_table, q_fx, kn_fx, vn_fx, c_fx_k, c_fx_v, l, n_new=t_new,
                                 lf_new=lf_new, lf_cache=c_fx_lf)
        rows = lambda o: o[:, :, :t_new].transpose(0, 2, 1, 3).reshape(rows_s, bw)
        x1, x1_bf = _post_mixers(xs, y_gm, rows(o_sb), rows(o_fx), gates, p, tm=tm_s)
        q_mem = _mem_query(x1_bf, p, tm=tm_s).reshape(n_req, t_new, MEM_W)
        q_mem = jnp.pad(q_mem, ((0, 0), (0, T_PAD - t_new), (0, 0)))
        o_mem = _mem_sample(q_mem, c_mem_k, c_mem_v, l)[:, :t_new].reshape(rows_s, MEM_W)
        x2, x2_bf = _post_mem(x1, o_mem, p, tm=tm_s)
        xs, xs_bf = _ffn(x2, x2_bf, p, tm=tm_s)
        heads = lambda a: a.reshape(n_req, t_new, N_HEADS, HEAD_DIM)
        outs['gmv_s'].append(v.reshape(n_req, t_new, bw))
        outs['sbk_s'].append(heads(kv[:, 0:bw]))
        outs['sbv_s'].append(heads(kv[:, bw:2 * bw]))
        outs['fxk_s'].append(heads(kv[:, 2 * bw:3 * bw]))
        outs['fxv_s'].append(heads(kv[:, 3 * bw:4 * bw]))
        outs['fxf_s'].append(lf_s)

    st = jnp.stack
    return (xp.reshape(batch, seq, D_MODEL), xs.reshape(n_req, t_new, D_MODEL),
            st(outs['sbk_p']), st(outs['sbv_p']), st(outs['fxk_p']), st(outs['fxv_p']), st(outs['fxf_p']),
            st(outs['mk_p']), st(outs['mv_p']), st(outs['gmv_s']), st(outs['sbk_s']), st(outs['sbv_s']),
            st(outs['fxk_s']), st(outs['fxv_s']), st(outs['fxf_s']))
```

```python
import functools

import jax
import jax.numpy as jnp
from jax import lax
from jax.experimental import pallas as pl
from jax.experimental.pallas import tpu as pltpu

F32 = jnp.float32
BF16 = jnp.bfloat16

D_MODEL = 2048
DEPTH = 2
HEAD_DIM = 128
N_HEADS = 8
BRANCH_W = N_HEADS * HEAD_DIM
GM_CHUNK = 128
MEM_TOKENS = 256
MEM_HEADS = 4
MEM_W = MEM_HEADS * HEAD_DIM
PAGE_SIZE = 128
ALPHA = (2.0 * DEPTH) ** 0.25
LN_EPS = 1e-5
ATTN_SCALE = HEAD_DIM ** -0.5

LANES = 128
SUBLANES = 8
VMEM_CAP = 56 << 20
PAGES_PER_STEP = 8
T_PAD = SUBLANES
TILES_PER_TRIP = 4


def _vmem(nbytes):
    return int(min(VMEM_CAP, max(16 << 20, 2 * nbytes + (8 << 20))))


def _nbytes(shape, dtype):
    n = 1
    for s in shape:
        n *= s
    return n * jnp.dtype(dtype).itemsize


def _sigmoid(x):
    return 1.0 / (1.0 + jnp.exp(-x))


def _neg_abs(x):
    bits = lax.bitcast_convert_type(x, jnp.int32) | jnp.int32(-2 ** 31)
    return lax.bitcast_convert_type(bits, F32)


def _log_sigmoid(x):
    return jnp.minimum(x, 0.0) - jnp.log(1.0 + jnp.exp(_neg_abs(x)))


def _layernorm(y, g, b):
    mu = jnp.mean(y, axis=-1, keepdims=True)
    d = y - mu
    var = jnp.mean(d * d, axis=-1, keepdims=True)
    return d * lax.rsqrt(var + LN_EPS) * g + b


def _split_dot(x, m_bf16):
    hi = x.astype(BF16)
    lo = (x - hi.astype(F32)).astype(BF16)
    return (jnp.dot(hi, m_bf16, preferred_element_type=F32)
            + jnp.dot(lo, m_bf16, preferred_element_type=F32))


def _mm_kernel(*refs, n_pairs, n_extra, n_out, epilogue, nk):
    pairs = refs[:2 * n_pairs]
    extra = refs[2 * n_pairs:2 * n_pairs + n_extra]
    outs = refs[2 * n_pairs + n_extra:2 * n_pairs + n_extra + n_out]

    def finish(accs):
        res = epilogue(*accs, *[e[...] for e in extra])
        for o, r in zip(outs, res):
            o[...] = r.astype(o.dtype)

    parts = [jnp.dot(pairs[2 * p][...], pairs[2 * p + 1][...], preferred_element_type=F32)
             for p in range(n_pairs)]
    if nk == 1:
        finish(parts)
    else:
        acc_ref = refs[-1]
        k = pl.program_id(2)

        @pl.when(k == 0)
        def _():
            acc_ref[...] = parts[0]

        @pl.when(k > 0)
        def _():
            acc_ref[...] += parts[0]

        @pl.when(k == nk - 1)
        def _():
            finish([acc_ref[...]])


def _mm(pairs, epilogue, out_dtypes, *, tm, tn, tk=None, extras=(), n_cols=None):
    m = pairs[0][0].shape[0]
    n = n_cols if n_cols is not None else pairs[0][1].shape[2]
    tm = min(tm, m)
    tn = min(tn, n)
    assert m % tm == 0 and n % tn == 0
    kdim = pairs[0][0].shape[1]
    nk = 1 if tk is None else kdim // tk
    assert nk == 1 or (len(pairs) == 1 and kdim % tk == 0)
    in_specs, args, nbytes = [], [], 0
    for x, w, layer, col_block in pairs:
        kk = x.shape[1] if nk == 1 else tk
        col_block = col_block or (lambda j: j)
        in_specs.append(pl.BlockSpec((tm, kk), lambda i, j, k: (i, k)))
        in_specs.append(pl.BlockSpec((None, kk, tn),
                                     lambda i, j, k, layer=layer, cb=col_block: (layer, k, cb(j))))
        args += [x, w]
        nbytes += _nbytes((tm, kk), x.dtype) + _nbytes((kk, tn), w.dtype)
    for e in extras:
        if e[0] == 'row':
            in_specs.append(pl.BlockSpec((1, tn), lambda i, j, k: (0, j)))
            nbytes += _nbytes((SUBLANES, tn), e[1].dtype)
        else:
            off = e[2]
            in_specs.append(pl.BlockSpec((tm, tn), lambda i, j, k, off=off: (i, j + off)))
            nbytes += _nbytes((tm, tn), e[1].dtype)
        args.append(e[1])
    out_shape = [jax.ShapeDtypeStruct((m, n), dt) for dt in out_dtypes]
    out_specs = [pl.BlockSpec((tm, tn), lambda i, j, k: (i, j)) for _ in out_dtypes]
    nbytes += sum(_nbytes((tm, tn), dt) for dt in out_dtypes)
    nbytes += (len(pairs) + 2) * _nbytes((tm, tn), F32)
    scratch = [pltpu.VMEM((tm, tn), F32)] if nk > 1 else []
    kern = functools.partial(_mm_kernel, n_pairs=len(pairs), n_extra=len(extras),
                             n_out=len(out_dtypes), epilogue=epilogue, nk=nk)
    return pl.pallas_call(
        kern,
        grid=(m // tm, n // tn, nk),
        in_specs=in_specs,
        out_specs=out_specs,
        out_shape=out_shape,
        scratch_shapes=scratch,
        compiler_params=pltpu.CompilerParams(
            dimension_semantics=("parallel", "parallel", "arbitrary"),
            vmem_limit_bytes=_vmem(nbytes)),
    )(*args)


def _kv_kernel(*refs):
    x_ref, w_ref = refs[:2]
    o32_ref, obf_ref = refs[-2:]
    acc = jnp.dot(x_ref[...], w_ref[...], preferred_element_type=F32)
    obf_ref[...] = acc.astype(obf_ref.dtype)
    for h in range(N_HEADS):
        o32_ref[pl.ds(h, acc.shape[0], stride=N_HEADS), :] = acc[:, h * HEAD_DIM:(h + 1) * HEAD_DIM]


def _kv_proj(x_bf, handle, segment, stacked_out, *, tm):
    w, layer = handle
    m, kdim = x_bf.shape
    tm = min(tm, m)
    in_specs = [pl.BlockSpec((tm, kdim), lambda i: (i, 0)),
                pl.BlockSpec((None, kdim, BRANCH_W), lambda i: (layer, 0, segment))]
    args = [x_bf, w]
    aliases = {}
    if stacked_out is not None:
        in_specs.append(pl.BlockSpec(memory_space=pl.ANY))
        args.append(stacked_out)
        aliases = {2: 0}
    nbytes = (_nbytes((tm, kdim), BF16) + _nbytes((kdim, BRANCH_W), BF16)
              + 2 * _nbytes((tm, BRANCH_W), F32) + _nbytes((tm, BRANCH_W), BF16))
    return pl.pallas_call(
        _kv_kernel,
        grid=(m // tm,),
        in_specs=in_specs,
        out_specs=[pl.BlockSpec((None, tm * N_HEADS, HEAD_DIM), lambda i: (layer, i, 0)),
                   pl.BlockSpec((tm, BRANCH_W), lambda i: (i, 0))],
        out_shape=[jax.ShapeDtypeStruct((DEPTH, m * N_HEADS, HEAD_DIM), F32),
                   jax.ShapeDtypeStruct((m, BRANCH_W), BF16)],
        input_output_aliases=aliases,
        compiler_params=pltpu.CompilerParams(dimension_semantics=("parallel",),
                                             vmem_limit_bytes=_vmem(nbytes)),
    )(*args)


def _epi_gelu(acc):
    return (jax.nn.gelu(acc),)


def _epi_gelu_ln(acc, g, b):
    return (_layernorm(jax.nn.gelu(acc), g, b),)


def _epi_scale(acc):
    return (acc * ATTN_SCALE,)


def _epi_dual(acc):
    return (acc, acc)


def _epi_logsig(acc, b):
    return (_log_sigmoid(acc + b),)


def _epi_sigmoid(acc):
    return (_sigmoid(acc),)


def _epi_merge(a_gm, a_sb, a_fx, g_gm, g_sb, g_fx):
    return (g_gm.astype(F32) * a_gm + g_sb.astype(F32) * a_sb + g_fx.astype(F32) * a_fx,)


def _epi_resid_ln(acc, resid, g, b):
    y = _layernorm(ALPHA * resid + acc, g, b)
    return (y, y)


def _epi_swiglu(a_gate, a_up):
    return (a_gate * _sigmoid(a_gate) * a_up,)


def _gmlp_kernel(u_ref, v_ref, ws_ref, bs_ref, o_ref, *, n_chunks, rows_per_request):
    r = lax.broadcasted_iota(jnp.int32, (GM_CHUNK, GM_CHUNK), 0)
    c = lax.broadcasted_iota(jnp.int32, (GM_CHUNK, GM_CHUNK), 1)
    mask = c <= r
    if rows_per_request is not None:
        mask = jnp.logical_and(mask, (r // rows_per_request) == (c // rows_per_request))
    for g in range(N_HEADS):
        w = jnp.where(mask, ws_ref[g], 0.0).astype(BF16)
        b = bs_ref[g]
        cols = slice(g * LANES, (g + 1) * LANES)
        for n in range(n_chunks):
            rows = slice(n * GM_CHUNK, (n + 1) * GM_CHUNK)
            mixed = jnp.dot(w, v_ref[rows, cols].astype(BF16), preferred_element_type=F32) + b
            o_ref[rows, cols] = (u_ref[rows, cols].astype(F32) * mixed).astype(o_ref.dtype)


def _gmlp(u, v, ws, bs, *, tm, rows_per_request=None):
    m = u.shape[0]
    tm = min(tm, m)
    nbytes = _nbytes((tm, BRANCH_W), BF16) * 2 + _nbytes((tm, BRANCH_W), F32)
    kern = functools.partial(_gmlp_kernel, n_chunks=tm // GM_CHUNK, rows_per_request=rows_per_request)
    return pl.pallas_call(
        kern,
        grid=(m // tm,),
        in_specs=[pl.BlockSpec((tm, BRANCH_W), lambda i: (i, 0)),
                  pl.BlockSpec((tm, BRANCH_W), lambda i: (i, 0)),
                  pl.BlockSpec((N_HEADS, GM_CHUNK, GM_CHUNK), lambda i: (0, 0, 0)),
                  pl.BlockSpec((N_HEADS, GM_CHUNK, 1), lambda i: (0, 0, 0))],
        out_specs=pl.BlockSpec((tm, BRANCH_W), lambda i: (i, 0)),
        out_shape=jax.ShapeDtypeStruct((m, BRANCH_W), BF16),
        compiler_params=pltpu.CompilerParams(dimension_semantics=("parallel",),
                                             vmem_limit_bytes=_vmem(nbytes)),
    )(u, v, ws, bs)


def _suffix_matrix(n):
    r = lax.broadcasted_iota(jnp.int32, (n, n), 0)
    c = lax.broadcasted_iota(jnp.int32, (n, n), 1)
    return jnp.where(r > c, 1.0, 0.0).astype(BF16)


def _causal_walk(step, init, i, tq, tk):
    groups = tq // tk
    base = i * groups
    states = []
    for g in range(groups):
        rows = slice(g * tk, (g + 1) * tk)
        st = step(rows, base + g, init(tk), True)
        for c in reversed(range(g)):
            st = step(rows, base + c, st, False)
        states.append(st)
    state = tuple(jnp.concatenate(parts, axis=0) for parts in zip(*states))
    full = slice(0, tq)

    def several(n, st):
        for c in range(TILES_PER_TRIP):
            st = step(full, base - 1 - c - TILES_PER_TRIP * n, st, False)
        return st

    state = lax.fori_loop(0, base // TILES_PER_TRIP, several, state)
    rest = base % TILES_PER_TRIP
    return lax.fori_loop(0, rest, lambda n, st: step(full, rest - 1 - n, st, False), state)


def _sb_prompt_kernel(q_ref, k_ref, v_ref, o_ref, *, tq, tk, sub):
    u = _suffix_matrix(sub)
    row = lax.broadcasted_iota(jnp.int32, (tk, tk), 0)
    col = lax.broadcasted_iota(jnp.int32, (tk, tk), 1)
    visible = col < row

    def step(rows, j, state, masked):
        carry, acc = state
        start = pl.multiple_of(j * tk, tk)
        z = lax.dot_general(q_ref[rows, :], k_ref[pl.ds(start, tk), :], (((1,), (1,)), ((), ())),
                            preferred_element_type=F32)
        ls = _log_sigmoid(z)
        lk = ls - z
        if masked:
            lk = jnp.where(visible, lk, 0.0)
        parts = [None] * (tk // sub)
        for c in reversed(range(tk // sub)):
            blk = lk[:, c * sub:(c + 1) * sub]
            parts[c] = jnp.dot(blk.astype(BF16), u, preferred_element_type=F32) + carry
            carry = carry + jnp.sum(blk, axis=1, keepdims=True)
        p = jnp.exp(ls + jnp.concatenate(parts, axis=1))
        if masked:
            p = jnp.where(visible, p, 0.0)
        acc = acc + jnp.dot(p.astype(BF16), v_ref[pl.ds(start, tk), :], preferred_element_type=F32)
        return carry, acc

    init = lambda n: (jnp.zeros((n, 1), F32), jnp.zeros((n, HEAD_DIM), F32))
    _, acc = _causal_walk(step, init, pl.program_id(1), tq, tk)
    o_ref[...] = acc.astype(o_ref.dtype)


def _sb_prompt(qq, k, v, *, q_off, tq, tk, sub):
    s = qq.shape[0]
    tq, tk = min(tq, s), min(tk, s)
    nbytes = 2 * _nbytes((s, HEAD_DIM), BF16) + 8 * _nbytes((tq, tk), F32)
    return pl.pallas_call(
        functools.partial(_sb_prompt_kernel, tq=tq, tk=tk, sub=min(sub, tk)),
        grid=(N_HEADS, s // tq),
        in_specs=[pl.BlockSpec((tq, HEAD_DIM), lambda h, i: (i, q_off + h)),
                  pl.BlockSpec((s, HEAD_DIM), lambda h, i: (0, h)),
                  pl.BlockSpec((s, HEAD_DIM), lambda h, i: (0, h))],
        out_specs=pl.BlockSpec((tq, HEAD_DIM), lambda h, i: (i, h)),
        out_shape=jax.ShapeDtypeStruct((s, BRANCH_W), BF16),
        compiler_params=pltpu.CompilerParams(dimension_semantics=("parallel", "arbitrary"),
                                             vmem_limit_bytes=_vmem(nbytes)),
    )(qq, k, v)


def _neg_cumsum_kernel(lf_ref, o_ref, carry_ref):
    @pl.when(pl.program_id(0) == 0)
    def _():
        carry_ref[...] = jnp.zeros_like(carry_ref)

    n = lf_ref.shape[0]
    r = lax.broadcasted_iota(jnp.int32, (n, n), 0)
    c = lax.broadcasted_iota(jnp.int32, (n, n), 1)
    incl = jnp.where(r <= c, 1.0, 0.0).astype(F32)
    lft = lf_ref[...].T
    cum = jnp.dot(lft, incl, preferred_element_type=F32, precision=lax.Precision.HIGHEST)
    cum = cum + carry_ref[...]
    carry_ref[...] = cum[:, n - 1:n]
    o_ref[...] = -cum[:N_HEADS, :]


def _neg_cumsum(logf_pad):
    s = logf_pad.shape[0]
    n = LANES
    return pl.pallas_call(
        _neg_cumsum_kernel,
        grid=(s // n,),
        in_specs=[pl.BlockSpec((n, LANES), lambda i: (i, 0))],
        out_specs=pl.BlockSpec((N_HEADS, n), lambda i: (0, i)),
        out_shape=jax.ShapeDtypeStruct((N_HEADS, s), F32),
        scratch_shapes=[pltpu.VMEM((LANES, 1), F32)],
        compiler_params=pltpu.CompilerParams(dimension_semantics=("arbitrary",)),
    )(logf_pad)


def _fox_prompt_kernel(q_ref, k_ref, v_ref, nc_ref, o_ref, *, tq, tk):
    row = lax.broadcasted_iota(jnp.int32, (tk, tk), 0)
    col = lax.broadcasted_iota(jnp.int32, (tk, tk), 1)
    visible = col <= row

    def step(rows, j, state, masked):
        m, l, acc = state
        start = pl.multiple_of(j * tk, tk)
        z = lax.dot_general(q_ref[rows, :], k_ref[pl.ds(start, tk), :], (((1,), (1,)), ((), ())),
                            preferred_element_type=F32)
        logits = z + nc_ref[:, pl.ds(start, tk)]
        if masked:
            logits = jnp.where(visible, logits, -jnp.inf)
        m_new = jnp.maximum(m, jnp.max(logits, axis=1, keepdims=True))
        alpha = jnp.exp(m - m_new)
        p = jnp.exp(logits - m_new)
        l = alpha * l + jnp.sum(p, axis=1, keepdims=True)
        acc = alpha * acc + jnp.dot(p.astype(BF16), v_ref[pl.ds(start, tk), :],
                                    preferred_element_type=F32)
        return m_new, l, acc

    init = lambda n: (jnp.full((n, 1), -jnp.inf, F32), jnp.zeros((n, 1), F32),
                      jnp.zeros((n, HEAD_DIM), F32))
    _, l, acc = _causal_walk(step, init, pl.program_id(1), tq, tk)
    o_ref[...] = (acc / l).astype(o_ref.dtype)


def _fox_prompt(qq, k, v, neg_c, *, q_off, tq, tk):
    s = qq.shape[0]
    tq, tk = min(tq, s), min(tk, s)
    nbytes = 2 * _nbytes((s, HEAD_DIM), BF16) + 8 * _nbytes((tq, tk), F32)
    return pl.pallas_call(
        functools.partial(_fox_prompt_kernel, tq=tq, tk=tk),
        grid=(N_HEADS, s // tq),
        in_specs=[pl.BlockSpec((tq, HEAD_DIM), lambda h, i: (i, q_off + h)),
                  pl.BlockSpec((s, HEAD_DIM), lambda h, i: (0, h)),
                  pl.BlockSpec((s, HEAD_DIM), lambda h, i: (0, h)),
                  pl.BlockSpec((None, 1, s), lambda h, i: (h, 0, 0))],
        out_specs=pl.BlockSpec((tq, HEAD_DIM), lambda h, i: (i, h)),
        out_shape=jax.ShapeDtypeStruct((s, BRANCH_W), BF16),
        compiler_params=pltpu.CompilerParams(dimension_semantics=("parallel", "arbitrary"),
                                             vmem_limit_bytes=_vmem(nbytes)),
    )(qq, k, v, neg_c.reshape(N_HEADS, 1, s))


def _pages_heads(refs):
    per_head = [jnp.concatenate([r[pl.ds(h, PAGE_SIZE, stride=N_HEADS), :] for r in refs], axis=0)
                for h in range(N_HEADS)]
    return jnp.stack(per_head).astype(BF16)


def _scores(q, kb):
    return jnp.einsum('htd,hsd->hts', q, kb, preferred_element_type=F32)


def _weighted(p, vb):
    return jnp.einsum('hts,hsd->htd', p.astype(BF16), vb, preferred_element_type=F32)


def _lane_blocks(x, width):
    return [x[:, g * width:(g + 1) * width] for g in range(x.shape[1] // width)]


def _suffix_within_pages(blocks, u):
    rows = blocks[0].shape[0]
    local = _split_dot(jnp.concatenate(blocks, axis=0), u)
    return [local[g * rows:(g + 1) * rows] for g in range(len(blocks))]


def _new_token_mask(inclusive, n_new):
    t = lax.broadcasted_iota(jnp.int32, (N_HEADS, T_PAD, PAGE_SIZE), 1)
    s = lax.broadcasted_iota(jnp.int32, (N_HEADS, T_PAD, PAGE_SIZE), 2)
    return jnp.logical_and((s <= t) if inclusive else (s < t), s < n_new)


def _sb_block(q, kb, vb, carry, acc, mask, u):
    z = _scores(q, kb)
    ls = _log_sigmoid(z)
    lk = ls - z
    if mask is not None:
        lk = jnp.where(mask, lk, 0.0)
    blocks = _lane_blocks(lk.reshape(N_HEADS * T_PAD, -1), PAGE_SIZE)
    local = _suffix_within_pages(blocks, u)
    parts = [None] * len(blocks)
    for g in reversed(range(len(blocks))):
        parts[g] = local[g] + carry
        carry = carry + jnp.sum(blocks[g], axis=1, keepdims=True)
    p = jnp.exp(ls + jnp.concatenate(parts, axis=1).reshape(ls.shape))
    if mask is not None:
        p = jnp.where(mask, p, 0.0)
    return carry, acc + _weighted(p, vb)


def _sb_sample_kernel(pt_ref, q_ref, kn_ref, vn_ref, *rest, n_new):
    kp = rest[:PAGES_PER_STEP]
    vp = rest[PAGES_PER_STEP:2 * PAGES_PER_STEP]
    o_ref, carry_ref, acc_ref = rest[2 * PAGES_PER_STEP:]
    j = pl.program_id(1)
    q = q_ref[...]
    u = _suffix_matrix(PAGE_SIZE)

    @pl.when(j == 0)
    def _():
        carry, acc = _sb_block(q, kn_ref[...], vn_ref[...], jnp.zeros((N_HEADS * T_PAD, 1), F32),
                               jnp.zeros((N_HEADS, T_PAD, HEAD_DIM), F32),
                               _new_token_mask(False, n_new), u)
        carry_ref[...] = carry
        acc_ref[...] = acc

    carry, acc = _sb_block(q, _pages_heads(kp), _pages_heads(vp), carry_ref[...], acc_ref[...], None, u)
    carry_ref[...] = carry
    acc_ref[...] = acc

    @pl.when(j == pl.num_programs(1) - 1)
    def _():
        o_ref[...] = acc.astype(o_ref.dtype)


def _fox_block(q, kb, vb, lf_pages, carry, m, l, acc, mask, u):
    local = _suffix_within_pages(lf_pages, u)
    later = jnp.zeros((N_HEADS, 1), F32)
    parts = [None] * len(lf_pages)
    for g in reversed(range(len(lf_pages))):
        parts[g] = local[g] + later
        later = later + jnp.sum(lf_pages[g], axis=1, keepdims=True)
    suffix = jnp.concatenate(parts, axis=1)
    logits = _scores(q, kb) + (carry + suffix[:, None, :])
    if mask is not None:
        logits = jnp.where(mask, logits, -jnp.inf)
    m_new = jnp.maximum(m, jnp.max(logits, axis=-1, keepdims=True))
    alpha = jnp.exp(m - m_new)
    p = jnp.exp(logits - m_new)
    l = alpha * l + jnp.sum(p, axis=-1, keepdims=True)
    acc = alpha * acc + _weighted(p, vb)
    return carry + later[:, None, :], m_new, l, acc


def _fox_sample_kernel(pt_ref, q_ref, kn_ref, vn_ref, lfn_ref, *rest, n_new):
    kp = rest[:PAGES_PER_STEP]
    vp = rest[PAGES_PER_STEP:2 * PAGES_PER_STEP]
    lfp = rest[2 * PAGES_PER_STEP:3 * PAGES_PER_STEP]
    o_ref, carry_ref, m_ref, l_ref, acc_ref = rest[3 * PAGES_PER_STEP:]
    j = pl.program_id(1)
    q = q_ref[...]
    u = _suffix_matrix(PAGE_SIZE)

    @pl.when(j == 0)
    def _():
        t = lax.broadcasted_iota(jnp.int32, (N_HEADS, T_PAD, PAGE_SIZE), 1)
        s = lax.broadcasted_iota(jnp.int32, (N_HEADS, T_PAD, PAGE_SIZE), 2)
        lfn = lfn_ref[...]
        suffix_new = _split_dot(lfn, u)
        carry0 = -jnp.sum(jnp.where(s == t, suffix_new[:, None, :], 0.0), axis=-1, keepdims=True)
        carry, m, l, acc = _fox_block(q, kn_ref[...], vn_ref[...], [lfn], carry0,
                                      jnp.full((N_HEADS, T_PAD, 1), -jnp.inf, F32),
                                      jnp.zeros((N_HEADS, T_PAD, 1), F32),
                                      jnp.zeros((N_HEADS, T_PAD, HEAD_DIM), F32),
                                      _new_token_mask(True, n_new), u)
        carry_ref[...] = carry
        m_ref[...] = m
        l_ref[...] = l
        acc_ref[...] = acc

    carry, m, l, acc = _fox_block(q, _pages_heads(kp), _pages_heads(vp), [r[...] for r in lfp],
                                  carry_ref[...], m_ref[...], l_ref[...], acc_ref[...], None, u)
    carry_ref[...] = carry
    m_ref[...] = m
    l_ref[...] = l
    acc_ref[...] = acc

    @pl.when(j == pl.num_programs(1) - 1)
    def _():
        o_ref[...] = (acc / l).astype(o_ref.dtype)


def _sample_attention(page_table, q, k_new, v_new, k_cache, v_cache, layer, *, n_new,
                      lf_new=None, lf_cache=None):
    n_req, n_pages = page_table.shape
    steps = n_pages // PAGES_PER_STEP
    fox = lf_cache is not None

    def page_map(g):
        def index(b, j, pt):
            return (layer, pt[b, n_pages - (j + 1) * PAGES_PER_STEP + g], 0, 0)
        return index

    small = lambda b, j, pt: (b, 0, 0, 0)
    in_specs = [pl.BlockSpec((None, N_HEADS, T_PAD, HEAD_DIM), small),
                pl.BlockSpec((None, N_HEADS, PAGE_SIZE, HEAD_DIM), small),
                pl.BlockSpec((None, N_HEADS, PAGE_SIZE, HEAD_DIM), small)]
    args = [q, k_new, v_new]
    if fox:
        in_specs.append(pl.BlockSpec((None, N_HEADS, PAGE_SIZE), lambda b, j, pt: (b, 0, 0)))
        args.append(lf_new)
    page_rows = PAGE_SIZE * N_HEADS
    for cache in (k_cache, v_cache):
        for g in range(PAGES_PER_STEP):
            in_specs.append(pl.BlockSpec((None, None, page_rows, HEAD_DIM), page_map(g)))
            args.append(cache)
    if fox:
        for g in range(PAGES_PER_STEP):
            in_specs.append(pl.BlockSpec((None, None, N_HEADS, PAGE_SIZE), page_map(g)))
            args.append(lf_cache)
    if fox:
        scratch = [pltpu.VMEM((N_HEADS, T_PAD, 1), F32)] * 3
    else:
        scratch = [pltpu.VMEM((N_HEADS * T_PAD, 1), F32)]
    scratch.append(pltpu.VMEM((N_HEADS, T_PAD, HEAD_DIM), F32))
    nbytes = 3 * PAGES_PER_STEP * _nbytes((page_rows, HEAD_DIM), F32)
    kern = functools.partial(_fox_sample_kernel if fox else _sb_sample_kernel, n_new=n_new)
    return pl.pallas_call(
        kern,
        grid_spec=pltpu.PrefetchScalarGridSpec(
            num_scalar_prefetch=1,
            grid=(n_req, steps),
            in_specs=in_specs,
            out_specs=pl.BlockSpec((None, N_HEADS, T_PAD, HEAD_DIM), small),
            scratch_shapes=scratch),
        out_shape=jax.ShapeDtypeStruct((n_req, N_HEADS, T_PAD, HEAD_DIM), BF16),
        compiler_params=pltpu.CompilerParams(dimension_semantics=("parallel", "arbitrary"),
                                             vmem_limit_bytes=_vmem(nbytes)),
    )(page_table, *args)


def _softmax_rows(s):
    m = jnp.max(s, axis=-1, keepdims=True)
    p = jnp.exp(s - m)
    return p, jnp.sum(p, axis=-1, keepdims=True)


def _mem_prompt_kernel(q_ref, k_ref, v_ref, o_ref):
    for h in range(MEM_HEADS):
        cols = slice(h * HEAD_DIM, (h + 1) * HEAD_DIM)
        s = lax.dot_general(q_ref[:, cols], k_ref[:, cols], (((1,), (1,)), ((), ())),
                            preferred_element_type=F32)
        p, l = _softmax_rows(s)
        o = jnp.dot(p.astype(BF16), v_ref[:, cols], preferred_element_type=F32)
        o_ref[:, cols] = (o / l).astype(o_ref.dtype)


def _mem_prompt(q, mk, mv, *, tm):
    m = q.shape[0]
    nbytes = 2 * _nbytes((tm, MEM_W), BF16) + 4 * _nbytes((tm, MEM_TOKENS), F32)
    return pl.pallas_call(
        _mem_prompt_kernel,
        grid=(m // tm,),
        in_specs=[pl.BlockSpec((tm, MEM_W), lambda i: (i, 0)),
                  pl.BlockSpec((MEM_TOKENS, MEM_W), lambda i: (0, 0)),
                  pl.BlockSpec((MEM_TOKENS, MEM_W), lambda i: (0, 0))],
        out_specs=pl.BlockSpec((tm, MEM_W), lambda i: (i, 0)),
        out_shape=jax.ShapeDtypeStruct((m, MEM_W), BF16),
        compiler_params=pltpu.CompilerParams(dimension_semantics=("parallel",),
                                             vmem_limit_bytes=_vmem(nbytes)),
    )(q, mk, mv)


def _mem_sample_kernel(q_ref, k_ref, v_ref, o_ref):
    for h in range(MEM_HEADS):
        cols = slice(h * HEAD_DIM, (h + 1) * HEAD_DIM)
        kh = k_ref[pl.ds(h, MEM_TOKENS, stride=MEM_HEADS), :].astype(BF16)
        vh = v_ref[pl.ds(h, MEM_TOKENS, stride=MEM_HEADS), :].astype(BF16)
        s = lax.dot_general(q_ref[:, cols], kh, (((1,), (1,)), ((), ())), preferred_element_type=F32)
        p, l = _softmax_rows(s)
        o = jnp.dot(p.astype(BF16), vh, preferred_element_type=F32)
        o_ref[:, cols] = (o / l).astype(o_ref.dtype)


def _mem_sample(q, k_cache, v_cache, layer):
    n_req = q.shape[0]
    rows = MEM_TOKENS * MEM_HEADS
    cache_spec = pl.BlockSpec((None, None, rows, HEAD_DIM), lambda b: (layer, b, 0, 0))
    return pl.pallas_call(
        _mem_sample_kernel,
        grid=(n_req,),
        in_specs=[pl.BlockSpec((None, T_PAD, MEM_W), lambda b: (b, 0, 0)), cache_spec, cache_spec],
        out_specs=pl.BlockSpec((None, T_PAD, MEM_W), lambda b: (b, 0, 0)),
        out_shape=jax.ShapeDtypeStruct((n_req, T_PAD, MEM_W), BF16),
        compiler_params=pltpu.CompilerParams(dimension_semantics=("parallel",)),
    )(q, k_cache, v_cache)


def _layer_weights(l, w_in, b_forget, gm_ln_g, gm_ln_b, gm_w_s, gm_b_s, w_br_gm, w_br_sb, w_br_fox,
                   w_mix_out, ln1_g, ln1_b, mem_w_q, mem_w_k, mem_w_v, mem_w_out, ln2_g, ln2_b,
                   ffn_w_gate, ffn_w_up, ffn_w_down, ln3_g, ln3_b):
    o_fg = 8 * BRANCH_W
    o_gate = o_fg + N_HEADS
    row = lambda a: a[l].reshape(1, -1).astype(F32)
    stacked = lambda a: (a.astype(BF16), l)
    pad_fg = jnp.zeros((DEPTH, D_MODEL, LANES - N_HEADS), BF16)
    return dict(
        w_main=stacked(w_in[:, :, :o_fg]),
        w_fg=(jnp.concatenate([w_in[:, :, o_fg:o_gate].astype(BF16), pad_fg], axis=2), l),
        b_fg=jnp.concatenate([b_forget[l].astype(F32), jnp.zeros((LANES - N_HEADS,), F32)]).reshape(1, LANES),
        w_gate=stacked(w_in[:, :, o_gate:]),
        gm_ln_g=row(gm_ln_g), gm_ln_b=row(gm_ln_b),
        gm_w_s=gm_w_s[l].astype(F32), gm_b_s=gm_b_s[l].astype(F32),
        w_br_gm=stacked(w_br_gm), w_br_sb=stacked(w_br_sb),
        w_br_fox=stacked(w_br_fox), w_mix_out=stacked(w_mix_out),
        ln1_g=row(ln1_g), ln1_b=row(ln1_b),
        mem_w_q=stacked(mem_w_q), mem_w_k=stacked(mem_w_k),
        mem_w_v=stacked(mem_w_v), mem_w_out=stacked(mem_w_out),
        ln2_g=row(ln2_g), ln2_b=row(ln2_b),
        ffn_w_gate=stacked(ffn_w_gate), ffn_w_up=stacked(ffn_w_up), ffn_w_down=stacked(ffn_w_down),
        ln3_g=row(ln3_g), ln3_b=row(ln3_b),
    )


def _pair(x, handle, col_block=None):
    w, layer = handle
    return (x, w, layer, col_block)


def _in_proj(x_bf, p, kv_stacked, *, tm):
    tn = 512
    seg = BRANCH_W // tn
    main = p['w_main']
    u, = _mm([_pair(x_bf, main)], _epi_gelu, [BF16], tm=tm, tn=tn, n_cols=BRANCH_W)
    v, = _mm([_pair(x_bf, main, lambda j: j + 1)], _epi_gelu_ln, [F32], tm=min(tm, 512), tn=BRANCH_W,
             n_cols=BRANCH_W, extras=[('row', p['gm_ln_g']), ('row', p['gm_ln_b'])])
    qq, = _mm([_pair(x_bf, main, lambda j: jnp.where(j < seg, j + 2 * seg, j + 4 * seg))],
              _epi_scale, [BF16], tm=tm, tn=tn, n_cols=2 * BRANCH_W)
    kv_stacked = kv_stacked or [None] * 4
    kv = [_kv_proj(x_bf, main, segment, prev, tm=tm)
          for segment, prev in zip((3, 4, 6, 7), kv_stacked)]
    logf, = _mm([_pair(x_bf, p['w_fg'])], _epi_logsig, [F32], tm=tm, tn=LANES, extras=[('row', p['b_fg'])])
    gates, = _mm([_pair(x_bf, p['w_gate'])], _epi_sigmoid, [BF16], tm=tm, tn=tn)
    return u, v, qq, [a for a, _ in kv], [b for _, b in kv], logf, gates


def _post_mixers(x, y_gm, y_sb, y_fx, gates, p, *, tm):
    gate_blocks = D_MODEL // 512
    h, = _mm([_pair(y_gm, p['w_br_gm']), _pair(y_sb, p['w_br_sb']), _pair(y_fx, p['w_br_fox'])],
             _epi_merge, [BF16], tm=min(tm, 512), tn=512,
             extras=[('tile', gates, 0), ('tile', gates, gate_blocks), ('tile', gates, 2 * gate_blocks)])
    return _mm([_pair(h, p['w_mix_out'])], _epi_resid_ln, [F32, BF16], tm=min(tm, 256), tn=D_MODEL,
               extras=[('tile', x, 0), ('row', p['ln1_g']), ('row', p['ln1_b'])])


def _mem_query(x1_bf, p, *, tm):
    q, = _mm([_pair(x1_bf, p['mem_w_q'])], _epi_scale, [BF16], tm=tm, tn=MEM_W)
    return q


def _post_mem(x1, o, p, *, tm):
    return _mm([_pair(o, p['mem_w_out'])], _epi_resid_ln, [F32, BF16], tm=min(tm, 256), tn=D_MODEL,
               extras=[('tile', x1, 0), ('row', p['ln2_g']), ('row', p['ln2_b'])])


def _ffn(x2, x2_bf, p, *, tm):
    d_ff = p['ffn_w_gate'][0].shape[2]
    hid, = _mm([_pair(x2_bf, p['ffn_w_gate']), _pair(x2_bf, p['ffn_w_up'])], _epi_swiglu, [BF16],
               tm=tm, tn=512)
    return _mm([_pair(hid, p['ffn_w_down'])], _epi_resid_ln, [F32, BF16], tm=min(tm, 512), tn=D_MODEL,
               tk=d_ff // 4,
               extras=[('tile', x2, 0), ('row', p['ln3_g']), ('row', p['ln3_b'])])


def _heads_first(a, n_req, t, pad_to, dtype):
    a = a.reshape(n_req, t, N_HEADS, HEAD_DIM).transpose(0, 2, 1, 3).astype(dtype)
    return jnp.pad(a, ((0, 0), (0, 0), (0, pad_to - t), (0, 0)))


def kernel(x_prompt, x_sample, cache_sb_k, cache_sb_v, cache_fox_k, cache_fox_v, cache_fox_logf,
           cache_mem_k, cache_mem_v, page_table, mem_prompt, w_in, b_forget, gm_ln_g, gm_ln_b,
           gm_w_s, gm_b_s, w_br_gm, w_br_sb, w_br_fox, w_mix_out, ln1_g, ln1_b, mem_w_q, mem_w_k,
           mem_w_v, mem_w_out, ln2_g, ln2_b, ffn_w_gate, ffn_w_up, ffn_w_down, ln3_g, ln3_b):
    batch, seq, _ = x_prompt.shape
    n_req, t_new, _ = x_sample.shape
    assert batch == 1 and t_new <= T_PAD
    n_pool = cache_sb_k.shape[1]
    rows_s = n_req * t_new
    bw = BRANCH_W

    page_rows = PAGE_SIZE * N_HEADS
    pool_view = lambda c: c.reshape(DEPTH, n_pool, page_rows, HEAD_DIM)
    c_sb_k, c_sb_v, c_fx_k, c_fx_v = map(pool_view, (cache_sb_k, cache_sb_v, cache_fox_k, cache_fox_v))
    c_fx_lf = cache_fox_logf.astype(F32).transpose(0, 1, 3, 2)
    mem_view = lambda c: c.reshape(DEPTH, n_req, MEM_TOKENS * MEM_HEADS, HEAD_DIM)
    c_mem_k, c_mem_v = mem_view(cache_mem_k), mem_view(cache_mem_v)
    mem_bf = mem_prompt.reshape(MEM_TOKENS, D_MODEL).astype(BF16)

    xp = x_prompt.reshape(seq, D_MODEL)
    xs = x_sample.reshape(rows_s, D_MODEL)
    xp_bf, xs_bf = xp.astype(BF16), xs.astype(BF16)
    outs = {k: [] for k in ('fxf_p', 'mk_p', 'mv_p', 'gmv_s', 'fxf_s')}
    kv_p = kv_s = None
    tm_p, tm_s, tq, tk = 1024, rows_s, 512, 512
    for l in range(DEPTH):
        p = _layer_weights(l, w_in, b_forget, gm_ln_g, gm_ln_b, gm_w_s, gm_b_s, w_br_gm, w_br_sb,
                           w_br_fox, w_mix_out, ln1_g, ln1_b, mem_w_q, mem_w_k, mem_w_v, mem_w_out,
                           ln2_g, ln2_b, ffn_w_gate, ffn_w_up, ffn_w_down, ln3_g, ln3_b)

        u, v, qq, kv_p, kv_bf, logf, gates = _in_proj(xp_bf, p, kv_p, tm=tm_p)
        y_gm = _gmlp(u, v, p['gm_w_s'], p['gm_b_s'].reshape(N_HEADS, GM_CHUNK, 1), tm=tm_p)
        y_sb = _sb_prompt(qq, kv_bf[0], kv_bf[1], q_off=0, tq=tq, tk=tk, sub=256)
        y_fx = _fox_prompt(qq, kv_bf[2], kv_bf[3], _neg_cumsum(logf), q_off=N_HEADS, tq=tq, tk=tk)
        x1, x1_bf = _post_mixers(xp, y_gm, y_sb, y_fx, gates, p, tm=tm_p)
        mk, mk_bf = _mm([_pair(mem_bf, p['mem_w_k'])], _epi_dual, [F32, BF16], tm=MEM_TOKENS, tn=MEM_W)
        mv, mv_bf = _mm([_pair(mem_bf, p['mem_w_v'])], _epi_dual, [F32, BF16], tm=MEM_TOKENS, tn=MEM_W)
        o_mem = _mem_prompt(_mem_query(x1_bf, p, tm=tm_p), mk_bf, mv_bf, tm=512)
        x2, x2_bf = _post_mem(x1, o_mem, p, tm=tm_p)
        xp, xp_bf = _ffn(x2, x2_bf, p, tm=tm_p)
        outs['fxf_p'].append(logf[:, :N_HEADS].reshape(batch, seq, N_HEADS))
        outs['mk_p'].append(mk.reshape(batch, MEM_TOKENS, MEM_HEADS, HEAD_DIM))
        outs['mv_p'].append(mv.reshape(batch, MEM_TOKENS, MEM_HEADS, HEAD_DIM))

        u, v, qq, kv_s, kv_bf, logf, gates = _in_proj(xs_bf, p, kv_s, tm=tm_s)
        ws_s = jnp.tile(p['gm_w_s'][:, :t_new, :t_new], (1, GM_CHUNK // t_new, GM_CHUNK // t_new))
        bs_s = jnp.tile(p['gm_b_s'][:, :t_new], (1, GM_CHUNK // t_new)).reshape(N_HEADS, GM_CHUNK, 1)
        y_gm = _gmlp(u, v, ws_s, bs_s, tm=tm_s, rows_per_request=t_new)
        hf = functools.partial(_heads_first, n_req=n_req, t=t_new, dtype=BF16)
        q_sb, q_fx = hf(qq[:, :bw], pad_to=T_PAD), hf(qq[:, bw:], pad_to=T_PAD)
        kn_sb, vn_sb, kn_fx, vn_fx = (hf(a, pad_to=PAGE_SIZE) for a in kv_bf)
        lf_s = logf[:, :N_HEADS].reshape(n_req, t_new, N_HEADS)
        lf_new = jnp.pad(lf_s.transpose(0, 2, 1), ((0, 0), (0, 0), (0, PAGE_SIZE - t_new)))
        o_sb = _sample_attention(page_table, q_sb, kn_sb, vn_sb, c_sb_k, c_sb_v, l, n_new=t_new)
        o_fx = _sample_attention(page_table, q_fx, kn_fx, vn_fx, c_fx_k, c_fx_v, l, n_new=t_new,
                                 lf_new=lf_new, lf_cache=c_fx_lf)
        rows = lambda o: o[:, :, :t_new].transpose(0, 2, 1, 3).reshape(rows_s, bw)
        x1, x1_bf = _post_mixers(xs, y_gm, rows(o_sb), rows(o_fx), gates, p, tm=tm_s)
        q_mem = _mem_query(x1_bf, p, tm=tm_s).reshape(n_req, t_new, MEM_W)
        q_mem = jnp.pad(q_mem, ((0, 0), (0, T_PAD - t_new), (0, 0)))
        o_mem = _mem_sample(q_mem, c_mem_k, c_mem_v, l)[:, :t_new].reshape(rows_s, MEM_W)
        x2, x2_bf = _post_mem(x1, o_mem, p, tm=tm_s)
        xs, xs_bf = _ffn(x2, x2_bf, p, tm=tm_s)
        outs['gmv_s'].append(v.reshape(n_req, t_new, bw))
        outs['fxf_s'].append(lf_s)

    st = jnp.stack
    kv_p = [a.reshape(DEPTH, batch, seq, N_HEADS, HEAD_DIM) for a in kv_p]
    kv_s = [a.reshape(DEPTH, n_req, t_new, N_HEADS, HEAD_DIM) for a in kv_s]
    return (xp.reshape(batch, seq, D_MODEL), xs.reshape(n_req, t_new, D_MODEL),
            *kv_p, st(outs['fxf_p']), st(outs['mk_p']), st(outs['mv_p']), st(outs['gmv_s']),
            *kv_s, st(outs['fxf_s']))
```

```python
import functools

import jax
import jax.numpy as jnp
from jax import lax
from jax.experimental import pallas as pl
from jax.experimental.pallas import tpu as pltpu

F32 = jnp.float32
BF16 = jnp.bfloat16

D_MODEL = 2048
DEPTH = 2
HEAD_DIM = 128
N_HEADS = 8
BRANCH_W = N_HEADS * HEAD_DIM
GM_CHUNK = 128
MEM_TOKENS = 256
MEM_HEADS = 4
MEM_W = MEM_HEADS * HEAD_DIM
PAGE_SIZE = 128
ALPHA = (2.0 * DEPTH) ** 0.25
LN_EPS = 1e-5
ATTN_SCALE = HEAD_DIM ** -0.5

LANES = 128
SUBLANES = 8
VMEM_CAP = 56 << 20
PAGES_PER_STEP = 16
T_PAD = SUBLANES
TILES_PER_TRIP = 4
SUB = 256


def _vmem(nbytes):
    return int(min(VMEM_CAP, max(16 << 20, 2 * nbytes + (8 << 20))))


def _nbytes(shape, dtype):
    n = 1
    for s in shape:
        n *= s
    return n * jnp.dtype(dtype).itemsize


def _sigmoid(x):
    return 1.0 / (1.0 + jnp.exp(-x))


def _neg_abs(x):
    bits = lax.bitcast_convert_type(x, jnp.int32) | jnp.int32(-2 ** 31)
    return lax.bitcast_convert_type(bits, F32)


def _log_sigmoid(x):
    return jnp.minimum(x, 0.0) - jnp.log(1.0 + jnp.exp(_neg_abs(x)))


def _layernorm(y, g, b):
    mu = jnp.mean(y, axis=-1, keepdims=True)
    d = y - mu
    var = jnp.mean(d * d, axis=-1, keepdims=True)
    return d * lax.rsqrt(var + LN_EPS) * g + b


def _split_dot(x, m_bf16):
    hi = x.astype(BF16)
    lo = (x - hi.astype(F32)).astype(BF16)
    return (jnp.dot(hi, m_bf16, preferred_element_type=F32)
            + jnp.dot(lo, m_bf16, preferred_element_type=F32))


def _dot_nt(x, w_t):
    return lax.dot_general(x, w_t, (((1,), (1,)), ((), ())), preferred_element_type=F32)


def _mm_kernel(*refs, transposed, extra_kinds, n_out, epilogue, sub_rows, sub_cols):
    n_pairs = len(transposed)
    pairs = refs[:2 * n_pairs]
    extra = refs[2 * n_pairs:2 * n_pairs + len(extra_kinds)]
    outs = refs[2 * n_pairs + len(extra_kinds):2 * n_pairs + len(extra_kinds) + n_out]
    tm, tn = outs[0].shape
    for r0 in range(0, tm, sub_rows):
        rows = slice(r0, r0 + sub_rows)
        for c0 in range(0, tn, sub_cols):
            cols = slice(c0, c0 + sub_cols)
            accs = [_dot_nt(pairs[2 * p][rows, :], pairs[2 * p + 1][cols, :]) if transposed[p] else
                    jnp.dot(pairs[2 * p][rows, :], pairs[2 * p + 1][:, cols], preferred_element_type=F32)
                    for p in range(n_pairs)]
            ex = [e[:, cols] if kind == 'row' else e[rows, cols] for e, kind in zip(extra, extra_kinds)]
            for o, r in zip(outs, epilogue(*accs, *ex)):
                o[rows, cols] = r.astype(o.dtype)


def _mm(pairs, epilogue, out_dtypes, *, tm, tn, extras=(), n_cols=None, sub_rows=None, sub_cols=None):
    m = pairs[0][0].shape[0]
    n = n_cols if n_cols is not None else pairs[0][1].shape[1 if pairs[0][4] else 2]
    tm = min(tm, m)
    tn = min(tn, n)
    sub_rows = min(sub_rows or tm, tm)
    sub_cols = min(sub_cols or tn, tn)
    assert m % tm == 0 and n % tn == 0 and tm % sub_rows == 0 and tn % sub_cols == 0
    w_mode = dict(pipeline_mode=pl.Buffered(1)) if tn == n and n_cols is None else {}
    in_specs, args, nbytes = [], [], 0
    for x, w, layer, col_block, transposed in pairs:
        kk = x.shape[1]
        col_block = col_block or (lambda j: j)
        in_specs.append(pl.BlockSpec((tm, kk), lambda i, j: (i, 0)))
        if transposed:
            in_specs.append(pl.BlockSpec((None, tn, kk),
                                         lambda i, j, layer=layer, cb=col_block: (layer, cb(j), 0), **w_mode))
        else:
            in_specs.append(pl.BlockSpec((None, kk, tn),
                                         lambda i, j, layer=layer, cb=col_block: (layer, 0, cb(j)), **w_mode))
        args += [x, w]
        nbytes += _nbytes((tm, kk), x.dtype) + _nbytes((kk, tn), w.dtype) // (2 if w_mode else 1)
    for e in extras:
        if e[0] == 'row':
            in_specs.append(pl.BlockSpec((1, tn), lambda i, j: (0, j)))
            nbytes += _nbytes((SUBLANES, tn), e[1].dtype)
        else:
            off = e[2]
            in_specs.append(pl.BlockSpec((tm, tn), lambda i, j, off=off: (i, j + off)))
            nbytes += _nbytes((tm, tn), e[1].dtype)
        args.append(e[1])
    out_shape = [jax.ShapeDtypeStruct((m, n), dt) for dt in out_dtypes]
    out_specs = [pl.BlockSpec((tm, tn), lambda i, j: (i, j)) for _ in out_dtypes]
    nbytes += sum(_nbytes((tm, tn), dt) for dt in out_dtypes)
    nbytes += (len(pairs) + 2) * _nbytes((sub_rows, sub_cols), F32)
    kern = functools.partial(_mm_kernel, transposed=tuple(bool(pr[4]) for pr in pairs),
                             extra_kinds=tuple(e[0] for e in extras),
                             n_out=len(out_dtypes), epilogue=epilogue, sub_rows=sub_rows, sub_cols=sub_cols)
    return pl.pallas_call(
        kern,
        grid=(m // tm, n // tn),
        in_specs=in_specs,
        out_specs=out_specs,
        out_shape=out_shape,
        compiler_params=pltpu.CompilerParams(
            dimension_semantics=("parallel", "parallel"),
            vmem_limit_bytes=_vmem(nbytes)),
    )(*args)


def _kv_kernel(*refs):
    x_ref, w_ref = refs[:2]
    o32_ref, obf_ref = refs[-2:]
    acc = _dot_nt(x_ref[...], w_ref[...])
    obf_ref[...] = acc.astype(obf_ref.dtype)
    for h in range(N_HEADS):
        o32_ref[pl.ds(h, acc.shape[0], stride=N_HEADS), :] = acc[:, h * HEAD_DIM:(h + 1) * HEAD_DIM]


def _kv_proj(x_bf, handle, segment, stacked_out, *, tm):
    w, layer, transposed = handle
    assert transposed
    m, kdim = x_bf.shape
    tm = min(tm, m)
    in_specs = [pl.BlockSpec((tm, kdim), lambda i: (i, 0)),
                pl.BlockSpec((None, BRANCH_W, kdim), lambda i: (layer, segment, 0))]
    args = [x_bf, w]
    aliases = {}
    if stacked_out is not None:
        in_specs.append(pl.BlockSpec(memory_space=pl.ANY))
        args.append(stacked_out)
        aliases = {2: 0}
    nbytes = (_nbytes((tm, kdim), BF16) + _nbytes((kdim, BRANCH_W), BF16)
              + 2 * _nbytes((tm, BRANCH_W), F32) + _nbytes((tm, BRANCH_W), BF16))
    return pl.pallas_call(
        _kv_kernel,
        grid=(m // tm,),
        in_specs=in_specs,
        out_specs=[pl.BlockSpec((None, tm * N_HEADS, HEAD_DIM), lambda i: (layer, i, 0)),
                   pl.BlockSpec((tm, BRANCH_W), lambda i: (i, 0))],
        out_shape=[jax.ShapeDtypeStruct((DEPTH, m * N_HEADS, HEAD_DIM), F32),
                   jax.ShapeDtypeStruct((m, BRANCH_W), BF16)],
        input_output_aliases=aliases,
        compiler_params=pltpu.CompilerParams(dimension_semantics=("parallel",),
                                             vmem_limit_bytes=_vmem(nbytes)),
    )(*args)


def _epi_gelu(acc):
    return (jax.nn.gelu(acc),)


def _epi_gelu_ln(acc, g, b):
    return (_layernorm(jax.nn.gelu(acc), g, b),)


def _epi_scale(acc):
    return (acc * ATTN_SCALE,)


def _epi_dual(acc):
    return (acc, acc)


def _epi_logsig(acc, b):
    return (_log_sigmoid(acc + b),)


def _epi_sigmoid(acc):
    return (_sigmoid(acc),)


def _epi_merge(a_gm, a_sb, a_fx, g_gm, g_sb, g_fx):
    return (g_gm.astype(F32) * a_gm + g_sb.astype(F32) * a_sb + g_fx.astype(F32) * a_fx,)


def _epi_resid_ln(acc, resid, g, b):
    y = _layernorm(ALPHA * resid + acc, g, b)
    return (y, y)


def _epi_swiglu(a_gate, a_up):
    return (a_gate * _sigmoid(a_gate) * a_up,)


def _gmlp_kernel(u_ref, v_ref, ws_ref, bs_ref, o_ref, *, n_chunks, rows_per_request):
    r = lax.broadcasted_iota(jnp.int32, (GM_CHUNK, GM_CHUNK), 0)
    c = lax.broadcasted_iota(jnp.int32, (GM_CHUNK, GM_CHUNK), 1)
    mask = c <= r
    if rows_per_request is not None:
        mask = jnp.logical_and(mask, (r // rows_per_request) == (c // rows_per_request))
    for g in range(N_HEADS):
        w = jnp.where(mask, ws_ref[g], 0.0).astype(BF16)
        b = bs_ref[g]
        cols = slice(g * LANES, (g + 1) * LANES)
        for n in range(n_chunks):
            rows = slice(n * GM_CHUNK, (n + 1) * GM_CHUNK)
            mixed = jnp.dot(w, v_ref[rows, cols].astype(BF16), preferred_element_type=F32) + b
            o_ref[rows, cols] = (u_ref[rows, cols].astype(F32) * mixed).astype(o_ref.dtype)


def _gmlp(u, v, ws, bs, *, tm, rows_per_request=None):
    m = u.shape[0]
    tm = min(tm, m)
    nbytes = _nbytes((tm, BRANCH_W), BF16) * 2 + _nbytes((tm, BRANCH_W), F32)
    kern = functools.partial(_gmlp_kernel, n_chunks=tm // GM_CHUNK, rows_per_request=rows_per_request)
    return pl.pallas_call(
        kern,
        grid=(m // tm,),
        in_specs=[pl.BlockSpec((tm, BRANCH_W), lambda i: (i, 0)),
                  pl.BlockSpec((tm, BRANCH_W), lambda i: (i, 0)),
                  pl.BlockSpec((N_HEADS, GM_CHUNK, GM_CHUNK), lambda i: (0, 0, 0)),
                  pl.BlockSpec((N_HEADS, GM_CHUNK, 1), lambda i: (0, 0, 0))],
        out_specs=pl.BlockSpec((tm, BRANCH_W), lambda i: (i, 0)),
        out_shape=jax.ShapeDtypeStruct((m, BRANCH_W), BF16),
        compiler_params=pltpu.CompilerParams(dimension_semantics=("parallel",),
                                             vmem_limit_bytes=_vmem(nbytes)),
    )(u, v, ws, bs)


def _suffix_matrix(n):
    r = lax.broadcasted_iota(jnp.int32, (n, n), 0)
    c = lax.broadcasted_iota(jnp.int32, (n, n), 1)
    return jnp.where(r > c, 1.0, 0.0).astype(BF16)


def _causal_walk(step, init, i, tq, tk):
    groups = tq // tk
    base = i * groups
    states = []
    for g in range(groups):
        rows = slice(g * tk, (g + 1) * tk)
        st = step(rows, base + g, init(tk), True)
        for c in reversed(range(g)):
            st = step(rows, base + c, st, False)
        states.append(st)
    state = tuple(jnp.concatenate(parts, axis=0) for parts in zip(*states))
    full = slice(0, tq)

    def several(n, st):
        for c in range(TILES_PER_TRIP):
            st = step(full, base - 1 - c - TILES_PER_TRIP * n, st, False)
        return st

    state = lax.fori_loop(0, base // TILES_PER_TRIP, several, state)
    rest = base % TILES_PER_TRIP
    return lax.fori_loop(0, rest, lambda n, st: step(full, rest - 1 - n, st, False), state)


def _sb_prompt_kernel(q_ref, k_ref, v_ref, o_ref, *, tq, tk, sub):
    u = _suffix_matrix(sub)
    row = lax.broadcasted_iota(jnp.int32, (tk, tk), 0)
    col = lax.broadcasted_iota(jnp.int32, (tk, tk), 1)
    visible = col < row

    def step(rows, j, state, masked):
        carry, acc = state
        start = pl.multiple_of(j * tk, tk)
        z = lax.dot_general(q_ref[rows, :], k_ref[pl.ds(start, tk), :], (((1,), (1,)), ((), ())),
                            preferred_element_type=F32)
        ls = _log_sigmoid(z)
        lk = ls - z
        if masked:
            lk = jnp.where(visible, lk, 0.0)
        parts = [None] * (tk // sub)
        for c in reversed(range(tk // sub)):
            blk = lk[:, c * sub:(c + 1) * sub]
            parts[c] = jnp.dot(blk.astype(BF16), u, preferred_element_type=F32) + carry
            carry = carry + jnp.sum(blk, axis=1, keepdims=True)
        p = jnp.exp(ls + jnp.concatenate(parts, axis=1))
        if masked:
            p = jnp.where(visible, p, 0.0)
        acc = acc + jnp.dot(p.astype(BF16), v_ref[pl.ds(start, tk), :], preferred_element_type=F32)
        return carry, acc

    init = lambda n: (jnp.zeros((n, 1), F32), jnp.zeros((n, HEAD_DIM), F32))
    _, acc = _causal_walk(step, init, pl.program_id(1), tq, tk)
    o_ref[...] = acc.astype(o_ref.dtype)


def _sb_prompt(qq, k, v, *, q_off, tq, tk, sub):
    s = qq.shape[0]
    tq, tk = min(tq, s), min(tk, s)
    nbytes = 2 * _nbytes((s, HEAD_DIM), BF16) + 8 * _nbytes((tq, tk), F32)
    return pl.pallas_call(
        functools.partial(_sb_prompt_kernel, tq=tq, tk=tk, sub=min(sub, tk)),
        grid=(N_HEADS, s // tq),
        in_specs=[pl.BlockSpec((tq, HEAD_DIM), lambda h, i: (i, q_off + h)),
                  pl.BlockSpec((s, HEAD_DIM), lambda h, i: (0, h)),
                  pl.BlockSpec((s, HEAD_DIM), lambda h, i: (0, h))],
        out_specs=pl.BlockSpec((tq, HEAD_DIM), lambda h, i: (i, h)),
        out_shape=jax.ShapeDtypeStruct((s, BRANCH_W), BF16),
        compiler_params=pltpu.CompilerParams(dimension_semantics=("parallel", "arbitrary"),
                                             vmem_limit_bytes=_vmem(nbytes)),
    )(qq, k, v)


def _neg_cumsum_kernel(lf_ref, o_ref, carry_ref):
    @pl.when(pl.program_id(0) == 0)
    def _():
        carry_ref[...] = jnp.zeros_like(carry_ref)

    n = LANES
    r = lax.broadcasted_iota(jnp.int32, (n, n), 0)
    c = lax.broadcasted_iota(jnp.int32, (n, n), 1)
    incl = jnp.where(r <= c, 1.0, 0.0).astype(F32)
    carry = carry_ref[...]
    for b in range(lf_ref.shape[0] // n):
        lft = lf_ref[b * n:(b + 1) * n, :].T
        cum = jnp.dot(lft, incl, preferred_element_type=F32, precision=lax.Precision.HIGHEST) + carry
        carry = cum[:, n - 1:n]
        o_ref[:, b * n:(b + 1) * n] = -cum[:N_HEADS, :]
    carry_ref[...] = carry


def _neg_cumsum(logf_pad):
    s = logf_pad.shape[0]
    n = min(s, 8 * LANES)
    return pl.pallas_call(
        _neg_cumsum_kernel,
        grid=(s // n,),
        in_specs=[pl.BlockSpec((n, LANES), lambda i: (i, 0))],
        out_specs=pl.BlockSpec((N_HEADS, n), lambda i: (0, i)),
        out_shape=jax.ShapeDtypeStruct((N_HEADS, s), F32),
        scratch_shapes=[pltpu.VMEM((LANES, 1), F32)],
        compiler_params=pltpu.CompilerParams(dimension_semantics=("arbitrary",)),
    )(logf_pad)


def _fox_prompt_kernel(q_ref, k_ref, v_ref, nc_ref, o_ref, *, tq, tk):
    row = lax.broadcasted_iota(jnp.int32, (tk, tk), 0)
    col = lax.broadcasted_iota(jnp.int32, (tk, tk), 1)
    visible = col <= row

    def step(rows, j, state, masked):
        m, l, acc = state
        start = pl.multiple_of(j * tk, tk)
        z = lax.dot_general(q_ref[rows, :], k_ref[pl.ds(start, tk), :], (((1,), (1,)), ((), ())),
                            preferred_element_type=F32)
        logits = z + nc_ref[:, pl.ds(start, tk)]
        if masked:
            logits = jnp.where(visible, logits, -jnp.inf)
        m_new = jnp.maximum(m, jnp.max(logits, axis=1, keepdims=True))
        alpha = jnp.exp(m - m_new)
        p = jnp.exp(logits - m_new)
        l = alpha * l + jnp.sum(p, axis=1, keepdims=True)
        acc = alpha * acc + jnp.dot(p.astype(BF16), v_ref[pl.ds(start, tk), :],
                                    preferred_element_type=F32)
        return m_new, l, acc

    init = lambda n: (jnp.full((n, 1), -jnp.inf, F32), jnp.zeros((n, 1), F32),
                      jnp.zeros((n, HEAD_DIM), F32))
    _, l, acc = _causal_walk(step, init, pl.program_id(1), tq, tk)
    o_ref[...] = (acc / l).astype(o_ref.dtype)


def _fox_prompt(qq, k, v, neg_c, *, q_off, tq, tk):
    s = qq.shape[0]
    tq, tk = min(tq, s), min(tk, s)
    nbytes = 2 * _nbytes((s, HEAD_DIM), BF16) + 8 * _nbytes((tq, tk), F32)
    return pl.pallas_call(
        functools.partial(_fox_prompt_kernel, tq=tq, tk=tk),
        grid=(N_HEADS, s // tq),
        in_specs=[pl.BlockSpec((tq, HEAD_DIM), lambda h, i: (i, q_off + h)),
                  pl.BlockSpec((s, HEAD_DIM), lambda h, i: (0, h)),
                  pl.BlockSpec((s, HEAD_DIM), lambda h, i: (0, h)),
                  pl.BlockSpec((None, 1, s), lambda h, i: (h, 0, 0))],
        out_specs=pl.BlockSpec((tq, HEAD_DIM), lambda h, i: (i, h)),
        out_shape=jax.ShapeDtypeStruct((s, BRANCH_W), BF16),
        compiler_params=pltpu.CompilerParams(dimension_semantics=("parallel", "arbitrary"),
                                             vmem_limit_bytes=_vmem(nbytes)),
    )(qq, k, v, neg_c.reshape(N_HEADS, 1, s))


def _pages_heads(refs):
    per_head = [jnp.concatenate([r[pl.ds(h, PAGE_SIZE, stride=N_HEADS), :] for r in refs], axis=0)
                for h in range(N_HEADS)]
    return jnp.stack(per_head).astype(BF16)


def _scores(q, kb):
    return jnp.einsum('htd,hsd->hts', q, kb, preferred_element_type=F32)


def _weighted(p, vb):
    return jnp.einsum('hts,hsd->htd', p.astype(BF16), vb, preferred_element_type=F32)


def _lane_blocks(x, width):
    return [x[:, g * width:(g + 1) * width] for g in range(x.shape[1] // width)]


def _suffix_within_pages(blocks, u):
    rows = blocks[0].shape[0]
    local = _split_dot(jnp.concatenate(blocks, axis=0), u)
    return [local[g * rows:(g + 1) * rows] for g in range(len(blocks))]


def _new_token_mask(inclusive, n_new):
    t = lax.broadcasted_iota(jnp.int32, (N_HEADS, T_PAD, PAGE_SIZE), 1)
    s = lax.broadcasted_iota(jnp.int32, (N_HEADS, T_PAD, PAGE_SIZE), 2)
    return jnp.logical_and((s <= t) if inclusive else (s < t), s < n_new)


def _sb_block(q, kb, vb, carry, acc, mask, u):
    z = _scores(q, kb)
    ls = _log_sigmoid(z)
    lk = ls - z
    if mask is not None:
        lk = jnp.where(mask, lk, 0.0)
    blocks = _lane_blocks(lk.reshape(N_HEADS * T_PAD, -1), PAGE_SIZE)
    local = _suffix_within_pages(blocks, u)
    parts = [None] * len(blocks)
    for g in reversed(range(len(blocks))):
        parts[g] = local[g] + carry
        carry = carry + jnp.sum(blocks[g], axis=1, keepdims=True)
    p = jnp.exp(ls + jnp.concatenate(parts, axis=1).reshape(ls.shape))
    if mask is not None:
        p = jnp.where(mask, p, 0.0)
    return carry, acc + _weighted(p, vb)


def _sb_sample_kernel(pt_ref, q_ref, kn_ref, vn_ref, *rest, n_new):
    kp = rest[:PAGES_PER_STEP]
    vp = rest[PAGES_PER_STEP:2 * PAGES_PER_STEP]
    o_ref, carry_ref, acc_ref = rest[2 * PAGES_PER_STEP:]
    j = pl.program_id(1)
    q = q_ref[...]
    u = _suffix_matrix(PAGE_SIZE)

    @pl.when(j == 0)
    def _():
        carry, acc = _sb_block(q, kn_ref[...], vn_ref[...], jnp.zeros((N_HEADS * T_PAD, 1), F32),
                               jnp.zeros((N_HEADS, T_PAD, HEAD_DIM), F32),
                               _new_token_mask(False, n_new), u)
        carry_ref[...] = carry
        acc_ref[...] = acc

    carry, acc = _sb_block(q, _pages_heads(kp), _pages_heads(vp), carry_ref[...], acc_ref[...], None, u)
    carry_ref[...] = carry
    acc_ref[...] = acc

    @pl.when(j == pl.num_programs(1) - 1)
    def _():
        o_ref[...] = acc.astype(o_ref.dtype)


def _fox_block(q, kb, vb, lf_pages, carry, m, l, acc, mask, u):
    local = _suffix_within_pages(lf_pages, u)
    later = jnp.zeros((N_HEADS, 1), F32)
    parts = [None] * len(lf_pages)
    for g in reversed(range(len(lf_pages))):
        parts[g] = local[g] + later
        later = later + jnp.sum(lf_pages[g], axis=1, keepdims=True)
    suffix = jnp.concatenate(parts, axis=1)
    logits = _scores(q, kb) + (carry + suffix[:, None, :])
    if mask is not None:
        logits = jnp.where(mask, logits, -jnp.inf)
    m_new = jnp.maximum(m, jnp.max(logits, axis=-1, keepdims=True))
    alpha = jnp.exp(m - m_new)
    p = jnp.exp(logits - m_new)
    l = alpha * l + jnp.sum(p, axis=-1, keepdims=True)
    acc = alpha * acc + _weighted(p, vb)
    return carry + later[:, None, :], m_new, l, acc


def _fox_sample_kernel(pt_ref, q_ref, kn_ref, vn_ref, lfn_ref, *rest, n_new):
    kp = rest[:PAGES_PER_STEP]
    vp = rest[PAGES_PER_STEP:2 * PAGES_PER_STEP]
    lfp = rest[2 * PAGES_PER_STEP:3 * PAGES_PER_STEP]
    o_ref, carry_ref, m_ref, l_ref, acc_ref = rest[3 * PAGES_PER_STEP:]
    j = pl.program_id(1)
    q = q_ref[...]
    u = _suffix_matrix(PAGE_SIZE)

    @pl.when(j == 0)
    def _():
        t = lax.broadcasted_iota(jnp.int32, (N_HEADS, T_PAD, PAGE_SIZE), 1)
        s = lax.broadcasted_iota(jnp.int32, (N_HEADS, T_PAD, PAGE_SIZE), 2)
        lfn = lfn_ref[...]
        suffix_new = _split_dot(lfn, u)
        carry0 = -jnp.sum(jnp.where(s == t, suffix_new[:, None, :], 0.0), axis=-1, keepdims=True)
        carry, m, l, acc = _fox_block(q, kn_ref[...], vn_ref[...], [lfn], carry0,
                                      jnp.full((N_HEADS, T_PAD, 1), -jnp.inf, F32),
                                      jnp.zeros((N_HEADS, T_PAD, 1), F32),
                                      jnp.zeros((N_HEADS, T_PAD, HEAD_DIM), F32),
                                      _new_token_mask(True, n_new), u)
        carry_ref[...] = carry
        m_ref[...] = m
        l_ref[...] = l
        acc_ref[...] = acc

    carry, m, l, acc = _fox_block(q, _pages_heads(kp), _pages_heads(vp), [r[...] for r in lfp],
                                  carry_ref[...], m_ref[...], l_ref[...], acc_ref[...], None, u)
    carry_ref[...] = carry
    m_ref[...] = m
    l_ref[...] = l
    acc_ref[...] = acc

    @pl.when(j == pl.num_programs(1) - 1)
    def _():
        o_ref[...] = (acc / l).astype(o_ref.dtype)


def _sample_attention(page_table, q, k_new, v_new, k_cache, v_cache, layer, *, n_new,
                      lf_new=None, lf_cache=None):
    n_req, n_pages = page_table.shape
    steps = n_pages // PAGES_PER_STEP
    fox = lf_cache is not None

    def page_map(g):
        def index(b, j, pt):
            return (layer, pt[b, n_pages - (j + 1) * PAGES_PER_STEP + g], 0, 0)
        return index

    small = lambda b, j, pt: (b, 0, 0, 0)
    in_specs = [pl.BlockSpec((None, N_HEADS, T_PAD, HEAD_DIM), small),
                pl.BlockSpec((None, N_HEADS, PAGE_SIZE, HEAD_DIM), small),
                pl.BlockSpec((None, N_HEADS, PAGE_SIZE, HEAD_DIM), small)]
    args = [q, k_new, v_new]
    if fox:
        in_specs.append(pl.BlockSpec((None, N_HEADS, PAGE_SIZE), lambda b, j, pt: (b, 0, 0)))
        args.append(lf_new)
    page_rows = PAGE_SIZE * N_HEADS
    for cache in (k_cache, v_cache):
        for g in range(PAGES_PER_STEP):
            in_specs.append(pl.BlockSpec((None, None, page_rows, HEAD_DIM), page_map(g)))
            args.append(cache)
    if fox:
        for g in range(PAGES_PER_STEP):
            in_specs.append(pl.BlockSpec((None, None, N_HEADS, PAGE_SIZE), page_map(g)))
            args.append(lf_cache)
    if fox:
        scratch = [pltpu.VMEM((N_HEADS, T_PAD, 1), F32)] * 3
    else:
        scratch = [pltpu.VMEM((N_HEADS * T_PAD, 1), F32)]
    scratch.append(pltpu.VMEM((N_HEADS, T_PAD, HEAD_DIM), F32))
    nbytes = 3 * PAGES_PER_STEP * _nbytes((page_rows, HEAD_DIM), F32)
    kern = functools.partial(_fox_sample_kernel if fox else _sb_sample_kernel, n_new=n_new)
    return pl.pallas_call(
        kern,
        grid_spec=pltpu.PrefetchScalarGridSpec(
            num_scalar_prefetch=1,
            grid=(n_req, steps),
            in_specs=in_specs,
            out_specs=pl.BlockSpec((None, N_HEADS, T_PAD, HEAD_DIM), small),
            scratch_shapes=scratch),
        out_shape=jax.ShapeDtypeStruct((n_req, N_HEADS, T_PAD, HEAD_DIM), BF16),
        compiler_params=pltpu.CompilerParams(dimension_semantics=("parallel", "arbitrary"),
                                             vmem_limit_bytes=_vmem(nbytes)),
    )(page_table, *args)


def _softmax_rows(s):
    m = jnp.max(s, axis=-1, keepdims=True)
    p = jnp.exp(s - m)
    return p, jnp.sum(p, axis=-1, keepdims=True)


def _mem_prompt_kernel(q_ref, k_ref, v_ref, o_ref):
    for h in range(MEM_HEADS):
        cols = slice(h * HEAD_DIM, (h + 1) * HEAD_DIM)
        s = lax.dot_general(q_ref[:, cols], k_ref[:, cols], (((1,), (1,)), ((), ())),
                            preferred_element_type=F32)
        p, l = _softmax_rows(s)
        o = jnp.dot(p.astype(BF16), v_ref[:, cols], preferred_element_type=F32)
        o_ref[:, cols] = (o / l).astype(o_ref.dtype)


def _mem_prompt(q, mk, mv, *, tm):
    m = q.shape[0]
    nbytes = 2 * _nbytes((tm, MEM_W), BF16) + 4 * _nbytes((tm, MEM_TOKENS), F32)
    return pl.pallas_call(
        _mem_prompt_kernel,
        grid=(m // tm,),
        in_specs=[pl.BlockSpec((tm, MEM_W), lambda i: (i, 0)),
                  pl.BlockSpec((MEM_TOKENS, MEM_W), lambda i: (0, 0)),
                  pl.BlockSpec((MEM_TOKENS, MEM_W), lambda i: (0, 0))],
        out_specs=pl.BlockSpec((tm, MEM_W), lambda i: (i, 0)),
        out_shape=jax.ShapeDtypeStruct((m, MEM_W), BF16),
        compiler_params=pltpu.CompilerParams(dimension_semantics=("parallel",),
                                             vmem_limit_bytes=_vmem(nbytes)),
    )(q, mk, mv)


def _mem_sample_kernel(q_ref, k_ref, v_ref, o_ref):
    for h in range(MEM_HEADS):
        cols = slice(h * HEAD_DIM, (h + 1) * HEAD_DIM)
        kh = k_ref[pl.ds(h, MEM_TOKENS, stride=MEM_HEADS), :].astype(BF16)
        vh = v_ref[pl.ds(h, MEM_TOKENS, stride=MEM_HEADS), :].astype(BF16)
        s = lax.dot_general(q_ref[:, cols], kh, (((1,), (1,)), ((), ())), preferred_element_type=F32)
        p, l = _softmax_rows(s)
        o = jnp.dot(p.astype(BF16), vh, preferred_element_type=F32)
        o_ref[:, cols] = (o / l).astype(o_ref.dtype)


def _mem_sample(q, k_cache, v_cache, layer):
    n_req = q.shape[0]
    rows = MEM_TOKENS * MEM_HEADS
    cache_spec = pl.BlockSpec((None, None, rows, HEAD_DIM), lambda b: (layer, b, 0, 0))
    return pl.pallas_call(
        _mem_sample_kernel,
        grid=(n_req,),
        in_specs=[pl.BlockSpec((None, T_PAD, MEM_W), lambda b: (b, 0, 0)), cache_spec, cache_spec],
        out_specs=pl.BlockSpec((None, T_PAD, MEM_W), lambda b: (b, 0, 0)),
        out_shape=jax.ShapeDtypeStruct((n_req, T_PAD, MEM_W), BF16),
        compiler_params=pltpu.CompilerParams(dimension_semantics=("parallel",)),
    )(q, k_cache, v_cache)


def _layer_weights(l, w_in, b_forget, gm_ln_g, gm_ln_b, gm_w_s, gm_b_s, w_br_gm, w_br_sb, w_br_fox,
                   w_mix_out, ln1_g, ln1_b, mem_w_q, mem_w_k, mem_w_v, mem_w_out, ln2_g, ln2_b,
                   ffn_w_gate, ffn_w_up, ffn_w_down, ln3_g, ln3_b):
    o_fg = 8 * BRANCH_W
    o_gate = o_fg + N_HEADS
    row = lambda a: a[l].reshape(1, -1).astype(F32)
    stacked = lambda a: (a.astype(BF16), l, False)
    w_in_t = jnp.swapaxes(w_in, 1, 2).astype(BF16)
    pad_fg = jnp.zeros((DEPTH, LANES - N_HEADS, D_MODEL), BF16)
    return dict(
        w_main=(w_in_t, l, True),
        w_fg=(jnp.concatenate([w_in_t[:, o_fg:o_gate], pad_fg], axis=1), l, True),
        b_fg=jnp.concatenate([b_forget[l].astype(F32), jnp.zeros((LANES - N_HEADS,), F32)]).reshape(1, LANES),
        w_gate=(w_in_t[:, o_gate:], l, True),
        gm_ln_g=row(gm_ln_g), gm_ln_b=row(gm_ln_b),
        gm_w_s=gm_w_s[l].astype(F32), gm_b_s=gm_b_s[l].astype(F32),
        w_br_gm=stacked(w_br_gm), w_br_sb=stacked(w_br_sb),
        w_br_fox=stacked(w_br_fox), w_mix_out=stacked(w_mix_out),
        ln1_g=row(ln1_g), ln1_b=row(ln1_b),
        mem_w_q=stacked(mem_w_q), mem_w_k=stacked(mem_w_k),
        mem_w_v=stacked(mem_w_v), mem_w_out=stacked(mem_w_out),
        ln2_g=row(ln2_g), ln2_b=row(ln2_b),
        ffn_w_gate=stacked(ffn_w_gate), ffn_w_up=stacked(ffn_w_up), ffn_w_down=stacked(ffn_w_down),
        ln3_g=row(ln3_g), ln3_b=row(ln3_b),
    )


def _pair(x, handle, col_block=None):
    w, layer, transposed = handle
    return (x, w, layer, col_block, transposed)


def _in_proj(x_bf, p, kv_stacked, *, tm):
    tn = 512
    seg = BRANCH_W // tn
    main = p['w_main']
    u, = _mm([_pair(x_bf, main)], _epi_gelu, [BF16], tm=tm, tn=tn, n_cols=BRANCH_W, sub_cols=SUB)
    v, = _mm([_pair(x_bf, main, lambda j: j + 1)], _epi_gelu_ln, [F32], tm=min(tm, 512), tn=BRANCH_W,
             n_cols=BRANCH_W, sub_rows=SUB, extras=[('row', p['gm_ln_g']), ('row', p['gm_ln_b'])])
    qq, = _mm([_pair(x_bf, main, lambda j: jnp.where(j < seg, j + 2 * seg, j + 4 * seg))],
              _epi_scale, [BF16], tm=tm, tn=tn, n_cols=2 * BRANCH_W, sub_cols=SUB)
    kv_stacked = kv_stacked or [None] * 4
    kv = [_kv_proj(x_bf, main, segment, prev, tm=tm)
          for segment, prev in zip((3, 4, 6, 7), kv_stacked)]
    logf, = _mm([_pair(x_bf, p['w_fg'])], _epi_logsig, [F32], tm=tm, tn=LANES, extras=[('row', p['b_fg'])])
    gates, = _mm([_pair(x_bf, p['w_gate'])], _epi_sigmoid, [BF16], tm=tm, tn=tn, sub_cols=SUB)
    return u, v, qq, [a for a, _ in kv], [b for _, b in kv], logf, gates


def _post_mixers(x, y_gm, y_sb, y_fx, gates, p, *, tm):
    gate_blocks = D_MODEL // 512
    h, = _mm([_pair(y_gm, p['w_br_gm']), _pair(y_sb, p['w_br_sb']), _pair(y_fx, p['w_br_fox'])],
             _epi_merge, [BF16], tm=min(tm, 512), tn=512, sub_cols=SUB,
             extras=[('tile', gates, 0), ('tile', gates, gate_blocks), ('tile', gates, 2 * gate_blocks)])
    return _mm([_pair(h, p['w_mix_out'])], _epi_resid_ln, [F32, BF16], tm=min(tm, 512), tn=D_MODEL,
               sub_rows=SUB, extras=[('tile', x, 0), ('row', p['ln1_g']), ('row', p['ln1_b'])])


def _mem_query(x1_bf, p, *, tm):
    q, = _mm([_pair(x1_bf, p['mem_w_q'])], _epi_scale, [BF16], tm=tm, tn=MEM_W)
    return q


def _post_mem(x1, o, p, *, tm):
    return _mm([_pair(o, p['mem_w_out'])], _epi_resid_ln, [F32, BF16], tm=min(tm, 512), tn=D_MODEL,
               sub_rows=SUB, extras=[('tile', x1, 0), ('row', p['ln2_g']), ('row', p['ln2_b'])])


def _ffn(x2, x2_bf, p, *, tm):
    hid, = _mm([_pair(x2_bf, p['ffn_w_gate']), _pair(x2_bf, p['ffn_w_up'])], _epi_swiglu, [BF16],
               tm=tm, tn=512, sub_cols=SUB)
    return _mm([_pair(hid, p['ffn_w_down'])], _epi_resid_ln, [F32, BF16], tm=min(tm, 256), tn=D_MODEL,
               sub_rows=SUB // 2, extras=[('tile', x2, 0), ('row', p['ln3_g']), ('row', p['ln3_b'])])


def _heads_first(a, n_req, t, pad_to, dtype):
    a = a.reshape(n_req, t, N_HEADS, HEAD_DIM).transpose(0, 2, 1, 3).astype(dtype)
    return jnp.pad(a, ((0, 0), (0, 0), (0, pad_to - t), (0, 0)))


def kernel(x_prompt, x_sample, cache_sb_k, cache_sb_v, cache_fox_k, cache_fox_v, cache_fox_logf,
           cache_mem_k, cache_mem_v, page_table, mem_prompt, w_in, b_forget, gm_ln_g, gm_ln_b,
           gm_w_s, gm_b_s, w_br_gm, w_br_sb, w_br_fox, w_mix_out, ln1_g, ln1_b, mem_w_q, mem_w_k,
           mem_w_v, mem_w_out, ln2_g, ln2_b, ffn_w_gate, ffn_w_up, ffn_w_down, ln3_g, ln3_b):
    batch, seq, _ = x_prompt.shape
    n_req, t_new, _ = x_sample.shape
    assert batch == 1 and t_new <= T_PAD
    n_pool = cache_sb_k.shape[1]
    rows_s = n_req * t_new
    bw = BRANCH_W

    page_rows = PAGE_SIZE * N_HEADS
    pool_view = lambda c: c.reshape(DEPTH, n_pool, page_rows, HEAD_DIM)
    c_sb_k, c_sb_v, c_fx_k, c_fx_v = map(pool_view, (cache_sb_k, cache_sb_v, cache_fox_k, cache_fox_v))
    c_fx_lf = cache_fox_logf.astype(F32).transpose(0, 1, 3, 2)
    mem_view = lambda c: c.reshape(DEPTH, n_req, MEM_TOKENS * MEM_HEADS, HEAD_DIM)
    c_mem_k, c_mem_v = mem_view(cache_mem_k), mem_view(cache_mem_v)
    mem_bf = mem_prompt.reshape(MEM_TOKENS, D_MODEL).astype(BF16)

    xp = x_prompt.reshape(seq, D_MODEL)
    xs = x_sample.reshape(rows_s, D_MODEL)
    xp_bf, xs_bf = xp.astype(BF16), xs.astype(BF16)
    outs = {k: [] for k in ('fxf_p', 'mk_p', 'mv_p', 'gmv_s', 'fxf_s')}
    kv_p = kv_s = None
    tm_p, tm_s, tq, tk = 1024, rows_s, 512, 512
    for l in range(DEPTH):
        p = _layer_weights(l, w_in, b_forget, gm_ln_g, gm_ln_b, gm_w_s, gm_b_s, w_br_gm, w_br_sb,
                           w_br_fox, w_mix_out, ln1_g, ln1_b, mem_w_q, mem_w_k, mem_w_v, mem_w_out,
                           ln2_g, ln2_b, ffn_w_gate, ffn_w_up, ffn_w_down, ln3_g, ln3_b)

        u, v, qq, kv_p, kv_bf, logf, gates = _in_proj(xp_bf, p, kv_p, tm=tm_p)
        y_gm = _gmlp(u, v, p['gm_w_s'], p['gm_b_s'].reshape(N_HEADS, GM_CHUNK, 1), tm=tm_p)
        y_sb = _sb_prompt(qq, kv_bf[0], kv_bf[1], q_off=0, tq=tq, tk=tk, sub=256)
        y_fx = _fox_prompt(qq, kv_bf[2], kv_bf[3], _neg_cumsum(logf), q_off=N_HEADS, tq=tq, tk=tk)
        x1, x1_bf = _post_mixers(xp, y_gm, y_sb, y_fx, gates, p, tm=tm_p)
        mk, mk_bf = _mm([_pair(mem_bf, p['mem_w_k'])], _epi_dual, [F32, BF16], tm=MEM_TOKENS, tn=MEM_W)
        mv, mv_bf = _mm([_pair(mem_bf, p['mem_w_v'])], _epi_dual, [F32, BF16], tm=MEM_TOKENS, tn=MEM_W)
        o_mem = _mem_prompt(_mem_query(x1_bf, p, tm=tm_p), mk_bf, mv_bf, tm=512)
        x2, x2_bf = _post_mem(x1, o_mem, p, tm=tm_p)
        xp, xp_bf = _ffn(x2, x2_bf, p, tm=tm_p)
        outs['fxf_p'].append(logf[:, :N_HEADS].reshape(batch, seq, N_HEADS))
        outs['mk_p'].append(mk.reshape(batch, MEM_TOKENS, MEM_HEADS, HEAD_DIM))
        outs['mv_p'].append(mv.reshape(batch, MEM_TOKENS, MEM_HEADS, HEAD_DIM))

        u, v, qq, kv_s, kv_bf, logf, gates = _in_proj(xs_bf, p, kv_s, tm=tm_s)
        ws_s = jnp.tile(p['gm_w_s'][:, :t_new, :t_new], (1, GM_CHUNK // t_new, GM_CHUNK // t_new))
        bs_s = jnp.tile(p['gm_b_s'][:, :t_new], (1, GM_CHUNK // t_new)).reshape(N_HEADS, GM_CHUNK, 1)
        y_gm = _gmlp(u, v, ws_s, bs_s, tm=tm_s, rows_per_request=t_new)
        hf = functools.partial(_heads_first, n_req=n_req, t=t_new, dtype=BF16)
        q_sb, q_fx = hf(qq[:, :bw], pad_to=T_PAD), hf(qq[:, bw:], pad_to=T_PAD)
        kn_sb, vn_sb, kn_fx, vn_fx = (hf(a, pad_to=PAGE_SIZE) for a in kv_bf)
        lf_s = logf[:, :N_HEADS].reshape(n_req, t_new, N_HEADS)
        lf_new = jnp.pad(lf_s.transpose(0, 2, 1), ((0, 0), (0, 0), (0, PAGE_SIZE - t_new)))
        o_sb = _sample_attention(page_table, q_sb, kn_sb, vn_sb, c_sb_k, c_sb_v, l, n_new=t_new)
        o_fx = _sample_attention(page_table, q_fx, kn_fx, vn_fx, c_fx_k, c_fx_v, l, n_new=t_new,
                                 lf_new=lf_new, lf_cache=c_fx_lf)
        rows = lambda o: o[:, :, :t_new].transpose(0, 2, 1, 3).reshape(rows_s, bw)
        x1, x1_bf = _post_mixers(xs, y_gm, rows(o_sb), rows(o_fx), gates, p, tm=tm_s)
        q_mem = _mem_query(x1_bf, p, tm=tm_s).reshape(n_req, t_new, MEM_W)
        q_mem = jnp.pad(q_mem, ((0, 0), (0, T_PAD - t_new), (0, 0)))
        o_mem = _mem_sample(q_mem, c_mem_k, c_mem_v, l)[:, :t_new].reshape(rows_s, MEM_W)
        x2, x2_bf = _post_mem(x1, o_mem, p, tm=tm_s)
        xs, xs_bf = _ffn(x2, x2_bf, p, tm=tm_s)
        outs['gmv_s'].append(v.reshape(n_req, t_new, bw))
        outs['fxf_s'].append(lf_s)

    st = jnp.stack
    kv_p = [a.reshape(DEPTH, batch, seq, N_HEADS, HEAD_DIM) for a in kv_p]
    kv_s = [a.reshape(DEPTH, n_req, t_new, N_HEADS, HEAD_DIM) for a in kv_s]
    return (xp.reshape(batch, seq, D_MODEL), xs.reshape(n_req, t_new, D_MODEL),
            *kv_p, st(outs['fxf_p']), st(outs['mk_p']), st(outs['mv_p']), st(outs['gmv_s']),
            *kv_s, st(outs['fxf_s']))
```

```python
import functools

import jax
import jax.numpy as jnp
from jax import lax
from jax.experimental import pallas as pl
from jax.experimental.pallas import tpu as pltpu

F32 = jnp.float32
BF16 = jnp.bfloat16

D_MODEL = 2048
DEPTH = 2
HEAD_DIM = 128
N_HEADS = 8
BRANCH_W = N_HEADS * HEAD_DIM
GM_CHUNK = 128
MEM_TOKENS = 256
MEM_HEADS = 4
MEM_W = MEM_HEADS * HEAD_DIM
PAGE_SIZE = 128
ALPHA = (2.0 * DEPTH) ** 0.25
LN_EPS = 1e-5
ATTN_SCALE = HEAD_DIM ** -0.5

LANES = 128
SUBLANES = 8
VMEM_CAP = 56 << 20
PAGES_PER_STEP = 16
T_PAD = SUBLANES
EXP_CUTOFF = 110.0
SUB = 256


def _vmem(nbytes):
    return int(min(VMEM_CAP, max(16 << 20, 2 * nbytes + (8 << 20))))


def _nbytes(shape, dtype):
    n = 1
    for s in shape:
        n *= s
    return n * jnp.dtype(dtype).itemsize


def _sigmoid(x):
    return 1.0 / (1.0 + jnp.exp(-x))


def _neg_abs(x):
    bits = lax.bitcast_convert_type(x, jnp.int32) | jnp.int32(-2 ** 31)
    return lax.bitcast_convert_type(bits, F32)


def _log_sigmoid(x):
    return jnp.minimum(x, 0.0) - jnp.log(1.0 + jnp.exp(_neg_abs(x)))


def _layernorm(y, g, b):
    mu = jnp.mean(y, axis=-1, keepdims=True)
    d = y - mu
    var = jnp.mean(d * d, axis=-1, keepdims=True)
    return d * lax.rsqrt(var + LN_EPS) * g + b


def _split_dot(x, m_bf16):
    hi = x.astype(BF16)
    lo = (x - hi.astype(F32)).astype(BF16)
    return (jnp.dot(hi, m_bf16, preferred_element_type=F32)
            + jnp.dot(lo, m_bf16, preferred_element_type=F32))


def _dot_nt(x, w_t):
    return lax.dot_general(x, w_t, (((1,), (1,)), ((), ())), preferred_element_type=F32)


def _mm_kernel(*refs, transposed, extra_kinds, n_out, epilogue, sub_rows, sub_cols):
    n_pairs = len(transposed)
    pairs = refs[:2 * n_pairs]
    extra = refs[2 * n_pairs:2 * n_pairs + len(extra_kinds)]
    outs = refs[2 * n_pairs + len(extra_kinds):2 * n_pairs + len(extra_kinds) + n_out]
    tm, tn = outs[0].shape
    for r0 in range(0, tm, sub_rows):
        rows = slice(r0, r0 + sub_rows)
        for c0 in range(0, tn, sub_cols):
            cols = slice(c0, c0 + sub_cols)
            accs = [_dot_nt(pairs[2 * p][rows, :], pairs[2 * p + 1][cols, :]) if transposed[p] else
                    jnp.dot(pairs[2 * p][rows, :], pairs[2 * p + 1][:, cols], preferred_element_type=F32)
                    for p in range(n_pairs)]
            ex = [e[:, cols] if kind == 'row' else e[rows, cols] for e, kind in zip(extra, extra_kinds)]
            for o, r in zip(outs, epilogue(*accs, *ex)):
                o[rows, cols] = r.astype(o.dtype)


def _mm(pairs, epilogue, out_dtypes, *, tm, tn, extras=(), n_cols=None, sub_rows=None, sub_cols=None):
    m = pairs[0][0].shape[0]
    n = n_cols if n_cols is not None else pairs[0][1].shape[1 if pairs[0][4] else 2]
    tm = min(tm, m)
    tn = min(tn, n)
    sub_rows = min(sub_rows or tm, tm)
    sub_cols = min(sub_cols or tn, tn)
    assert m % tm == 0 and n % tn == 0 and tm % sub_rows == 0 and tn % sub_cols == 0
    w_mode = dict(pipeline_mode=pl.Buffered(1)) if tn == n and n_cols is None else {}
    in_specs, args, nbytes = [], [], 0
    for x, w, layer, col_block, transposed in pairs:
        kk = x.shape[1]
        col_block = col_block or (lambda j: j)
        in_specs.append(pl.BlockSpec((tm, kk), lambda i, j: (i, 0)))
        if transposed:
            in_specs.append(pl.BlockSpec((None, tn, kk),
                                         lambda i, j, layer=layer, cb=col_block: (layer, cb(j), 0), **w_mode))
        else:
            in_specs.append(pl.BlockSpec((None, kk, tn),
                                         lambda i, j, layer=layer, cb=col_block: (layer, 0, cb(j)), **w_mode))
        args += [x, w]
        nbytes += _nbytes((tm, kk), x.dtype) + _nbytes((kk, tn), w.dtype) // (2 if w_mode else 1)
    for e in extras:
        if e[0] == 'row':
            in_specs.append(pl.BlockSpec((1, tn), lambda i, j: (0, j)))
            nbytes += _nbytes((SUBLANES, tn), e[1].dtype)
        else:
            off = e[2]
            in_specs.append(pl.BlockSpec((tm, tn), lambda i, j, off=off: (i, j + off)))
            nbytes += _nbytes((tm, tn), e[1].dtype)
        args.append(e[1])
    out_shape = [jax.ShapeDtypeStruct((m, n), dt) for dt in out_dtypes]
    out_specs = [pl.BlockSpec((tm, tn), lambda i, j: (i, j)) for _ in out_dtypes]
    nbytes += sum(_nbytes((tm, tn), dt) for dt in out_dtypes)
    nbytes += (len(pairs) + 2) * _nbytes((sub_rows, sub_cols), F32)
    kern = functools.partial(_mm_kernel, transposed=tuple(bool(pr[4]) for pr in pairs),
                             extra_kinds=tuple(e[0] for e in extras),
                             n_out=len(out_dtypes), epilogue=epilogue, sub_rows=sub_rows, sub_cols=sub_cols)
    return pl.pallas_call(
        kern,
        grid=(m // tm, n // tn),
        in_specs=in_specs,
        out_specs=out_specs,
        out_shape=out_shape,
        compiler_params=pltpu.CompilerParams(
            dimension_semantics=("parallel", "parallel"),
            vmem_limit_bytes=_vmem(nbytes)),
    )(*args)


def _kv_kernel(*refs):
    x_ref, w_ref = refs[:2]
    o32_ref, obf_ref = refs[-2:]
    acc = _dot_nt(x_ref[...], w_ref[...])
    obf_ref[...] = acc.astype(obf_ref.dtype)
    for h in range(N_HEADS):
        o32_ref[pl.ds(h, acc.shape[0], stride=N_HEADS), :] = acc[:, h * HEAD_DIM:(h + 1) * HEAD_DIM]


def _kv_proj(x_bf, handle, segment, stacked_out, *, tm):
    w, layer, transposed = handle
    assert transposed
    m, kdim = x_bf.shape
    tm = min(tm, m)
    in_specs = [pl.BlockSpec((tm, kdim), lambda i: (i, 0)),
                pl.BlockSpec((None, BRANCH_W, kdim), lambda i: (layer, segment, 0))]
    args = [x_bf, w]
    aliases = {}
    if stacked_out is not None:
        in_specs.append(pl.BlockSpec(memory_space=pl.ANY))
        args.append(stacked_out)
        aliases = {2: 0}
    nbytes = (_nbytes((tm, kdim), BF16) + _nbytes((kdim, BRANCH_W), BF16)
              + 2 * _nbytes((tm, BRANCH_W), F32) + _nbytes((tm, BRANCH_W), BF16))
    return pl.pallas_call(
        _kv_kernel,
        grid=(m // tm,),
        in_specs=in_specs,
        out_specs=[pl.BlockSpec((None, tm * N_HEADS, HEAD_DIM), lambda i: (layer, i, 0)),
                   pl.BlockSpec((tm, BRANCH_W), lambda i: (i, 0))],
        out_shape=[jax.ShapeDtypeStruct((DEPTH, m * N_HEADS, HEAD_DIM), F32),
                   jax.ShapeDtypeStruct((m, BRANCH_W), BF16)],
        input_output_aliases=aliases,
        compiler_params=pltpu.CompilerParams(dimension_semantics=("parallel",),
                                             vmem_limit_bytes=_vmem(nbytes)),
    )(*args)


def _epi_gelu(acc):
    return (jax.nn.gelu(acc),)


def _epi_gelu_ln(acc, g, b):
    return (_layernorm(jax.nn.gelu(acc), g, b),)


def _epi_scale(acc):
    return (acc * ATTN_SCALE,)


def _epi_dual(acc):
    return (acc, acc)


def _epi_logsig(acc, b):
    return (_log_sigmoid(acc + b),)


def _epi_sigmoid(acc):
    return (_sigmoid(acc),)


def _epi_merge(a_gm, a_sb, a_fx, g_gm, g_sb, g_fx):
    return (g_gm.astype(F32) * a_gm + g_sb.astype(F32) * a_sb + g_fx.astype(F32) * a_fx,)


def _epi_resid_ln(acc, resid, g, b):
    y = _layernorm(ALPHA * resid + acc, g, b)
    return (y, y)


def _epi_swiglu(a_gate, a_up):
    return (a_gate * _sigmoid(a_gate) * a_up,)


def _gmlp_kernel(u_ref, v_ref, ws_ref, bs_ref, o_ref, *, n_chunks, rows_per_request):
    r = lax.broadcasted_iota(jnp.int32, (GM_CHUNK, GM_CHUNK), 0)
    c = lax.broadcasted_iota(jnp.int32, (GM_CHUNK, GM_CHUNK), 1)
    mask = c <= r
    if rows_per_request is not None:
        mask = jnp.logical_and(mask, (r // rows_per_request) == (c // rows_per_request))
    for g in range(N_HEADS):
        w = jnp.where(mask, ws_ref[g], 0.0).astype(BF16)
        b = bs_ref[g]
        cols = slice(g * LANES, (g + 1) * LANES)
        for n in range(n_chunks):
            rows = slice(n * GM_CHUNK, (n + 1) * GM_CHUNK)
            mixed = jnp.dot(w, v_ref[rows, cols].astype(BF16), preferred_element_type=F32) + b
            o_ref[rows, cols] = (u_ref[rows, cols].astype(F32) * mixed).astype(o_ref.dtype)


def _gmlp(u, v, ws, bs, *, tm, rows_per_request=None):
    m = u.shape[0]
    tm = min(tm, m)
    nbytes = _nbytes((tm, BRANCH_W), BF16) * 2 + _nbytes((tm, BRANCH_W), F32)
    kern = functools.partial(_gmlp_kernel, n_chunks=tm // GM_CHUNK, rows_per_request=rows_per_request)
    return pl.pallas_call(
        kern,
        grid=(m // tm,),
        in_specs=[pl.BlockSpec((tm, BRANCH_W), lambda i: (i, 0)),
                  pl.BlockSpec((tm, BRANCH_W), lambda i: (i, 0)),
                  pl.BlockSpec((N_HEADS, GM_CHUNK, GM_CHUNK), lambda i: (0, 0, 0)),
                  pl.BlockSpec((N_HEADS, GM_CHUNK, 1), lambda i: (0, 0, 0))],
        out_specs=pl.BlockSpec((tm, BRANCH_W), lambda i: (i, 0)),
        out_shape=jax.ShapeDtypeStruct((m, BRANCH_W), BF16),
        compiler_params=pltpu.CompilerParams(dimension_semantics=("parallel",),
                                             vmem_limit_bytes=_vmem(nbytes)),
    )(u, v, ws, bs)


def _suffix_matrix(n):
    r = lax.broadcasted_iota(jnp.int32, (n, n), 0)
    c = lax.broadcasted_iota(jnp.int32, (n, n), 1)
    return jnp.where(r > c, 1.0, 0.0).astype(BF16)


def _causal_walk(step, init, live, i, tq, tk):
    groups = tq // tk
    base = i * groups
    states = []
    for g in range(groups):
        rows = slice(g * tk, (g + 1) * tk)
        st = step(rows, base + g, init(tk), True)
        for c in reversed(range(g)):
            st = step(rows, base + c, st, False)
        states.append(st)
    state = tuple(jnp.concatenate(parts, axis=0) for parts in zip(*states))
    full = slice(0, tq)

    def pending(c):
        n, st = c
        return jnp.logical_and(n < base, live(st, jnp.maximum(base - 1 - n, 0)))

    def one(c):
        n, st = c
        return n + 1, step(full, base - 1 - n, st, False)

    return lax.while_loop(pending, one, (jnp.int32(0), state))[1]


def _sb_prompt_kernel(q_ref, k_ref, v_ref, o_ref, *, tq, tk, sub):
    u = _suffix_matrix(sub)
    row = lax.broadcasted_iota(jnp.int32, (tk, tk), 0)
    col = lax.broadcasted_iota(jnp.int32, (tk, tk), 1)
    visible = col < row

    def step(rows, j, state, masked):
        carry, acc = state
        start = pl.multiple_of(j * tk, tk)
        z = lax.dot_general(q_ref[rows, :], k_ref[pl.ds(start, tk), :], (((1,), (1,)), ((), ())),
                            preferred_element_type=F32)
        ls = _log_sigmoid(z)
        lk = ls - z
        if masked:
            lk = jnp.where(visible, lk, 0.0)
        parts = [None] * (tk // sub)
        for c in reversed(range(tk // sub)):
            blk = lk[:, c * sub:(c + 1) * sub]
            parts[c] = jnp.dot(blk.astype(BF16), u, preferred_element_type=F32) + carry
            carry = carry + jnp.sum(blk, axis=1, keepdims=True)
        p = jnp.exp(ls + jnp.concatenate(parts, axis=1))
        if masked:
            p = jnp.where(visible, p, 0.0)
        acc = acc + jnp.dot(p.astype(BF16), v_ref[pl.ds(start, tk), :], preferred_element_type=F32)
        return carry, acc

    def live(state, j):
        return jnp.max(state[0]) > -EXP_CUTOFF

    init = lambda n: (jnp.zeros((n, 1), F32), jnp.zeros((n, HEAD_DIM), F32))
    _, acc = _causal_walk(step, init, live, pl.program_id(1), tq, tk)
    o_ref[...] = acc.astype(o_ref.dtype)


def _sb_prompt(qq, k, v, *, q_off, tq, tk, sub):
    s = qq.shape[0]
    tq, tk = min(tq, s), min(tk, s)
    nbytes = 2 * _nbytes((s, HEAD_DIM), BF16) + 8 * _nbytes((tq, tk), F32)
    return pl.pallas_call(
        functools.partial(_sb_prompt_kernel, tq=tq, tk=tk, sub=min(sub, tk)),
        grid=(N_HEADS, s // tq),
        in_specs=[pl.BlockSpec((tq, HEAD_DIM), lambda h, i: (i, q_off + h)),
                  pl.BlockSpec((s, HEAD_DIM), lambda h, i: (0, h)),
                  pl.BlockSpec((s, HEAD_DIM), lambda h, i: (0, h))],
        out_specs=pl.BlockSpec((tq, HEAD_DIM), lambda h, i: (i, h)),
        out_shape=jax.ShapeDtypeStruct((s, BRANCH_W), BF16),
        compiler_params=pltpu.CompilerParams(dimension_semantics=("parallel", "arbitrary"),
                                             vmem_limit_bytes=_vmem(nbytes)),
    )(qq, k, v)


def _neg_cumsum_kernel(lf_ref, o_ref, carry_ref):
    @pl.when(pl.program_id(0) == 0)
    def _():
        carry_ref[...] = jnp.zeros_like(carry_ref)

    n = LANES
    r = lax.broadcasted_iota(jnp.int32, (n, n), 0)
    c = lax.broadcasted_iota(jnp.int32, (n, n), 1)
    incl = jnp.where(r <= c, 1.0, 0.0).astype(F32)
    carry = carry_ref[...]
    for b in range(lf_ref.shape[0] // n):
        lft = lf_ref[b * n:(b + 1) * n, :].T
        cum = jnp.dot(lft, incl, preferred_element_type=F32, precision=lax.Precision.HIGHEST) + carry
        carry = cum[:, n - 1:n]
        o_ref[:, b * n:(b + 1) * n] = -cum[:N_HEADS, :]
    carry_ref[...] = carry


def _neg_cumsum(logf_pad):
    s = logf_pad.shape[0]
    n = min(s, 8 * LANES)
    return pl.pallas_call(
        _neg_cumsum_kernel,
        grid=(s // n,),
        in_specs=[pl.BlockSpec((n, LANES), lambda i: (i, 0))],
        out_specs=pl.BlockSpec((N_HEADS, n), lambda i: (0, i)),
        out_shape=jax.ShapeDtypeStruct((N_HEADS, s), F32),
        scratch_shapes=[pltpu.VMEM((LANES, 1), F32)],
        compiler_params=pltpu.CompilerParams(dimension_semantics=("arbitrary",)),
    )(logf_pad)


def _row_norms(x):
    x = x.astype(F32)
    return jnp.sqrt(jnp.sum(x * x, axis=1, keepdims=True))


def _fox_prompt_kernel(q_ref, k_ref, v_ref, nc_ref, o_ref, kmax_ref, *, tq, tk):
    row = lax.broadcasted_iota(jnp.int32, (tk, tk), 0)
    col = lax.broadcasted_iota(jnp.int32, (tk, tk), 1)
    visible = col <= row

    @pl.when(pl.program_id(1) == 0)
    def _():
        longest = jnp.zeros((tk, 1), F32)
        for c in range(k_ref.shape[0] // tk):
            longest = jnp.maximum(longest, _row_norms(k_ref[c * tk:(c + 1) * tk, :]))
        kmax_ref[...] = jnp.max(longest, axis=0, keepdims=True)

    score_bound = _row_norms(q_ref[...]) * kmax_ref[...]

    def step(rows, j, state, masked):
        m, l, acc = state
        start = pl.multiple_of(j * tk, tk)
        z = lax.dot_general(q_ref[rows, :], k_ref[pl.ds(start, tk), :], (((1,), (1,)), ((), ())),
                            preferred_element_type=F32)
        logits = z + nc_ref[:, pl.ds(start, tk)]
        if masked:
            logits = jnp.where(visible, logits, -jnp.inf)
        m_new = jnp.maximum(m, jnp.max(logits, axis=1, keepdims=True))
        alpha = jnp.exp(m - m_new)
        p = jnp.exp(logits - m_new)
        l = alpha * l + jnp.sum(p, axis=1, keepdims=True)
        acc = alpha * acc + jnp.dot(p.astype(BF16), v_ref[pl.ds(start, tk), :],
                                    preferred_element_type=F32)
        return m_new, l, acc

    def live(state, j):
        start = pl.multiple_of(j * tk, tk)
        newest = jnp.max(nc_ref[:, pl.ds(start, tk)])
        return jnp.max(score_bound - state[0]) + newest > -EXP_CUTOFF

    init = lambda n: (jnp.full((n, 1), -jnp.inf, F32), jnp.zeros((n, 1), F32),
                      jnp.zeros((n, HEAD_DIM), F32))
    _, l, acc = _causal_walk(step, init, live, pl.program_id(1), tq, tk)
    o_ref[...] = (acc / l).astype(o_ref.dtype)


def _fox_prompt(qq, k, v, neg_c, *, q_off, tq, tk):
    s = qq.shape[0]
    tq, tk = min(tq, s), min(tk, s)
    nbytes = 2 * _nbytes((s, HEAD_DIM), BF16) + 8 * _nbytes((tq, tk), F32)
    return pl.pallas_call(
        functools.partial(_fox_prompt_kernel, tq=tq, tk=tk),
        grid=(N_HEADS, s // tq),
        in_specs=[pl.BlockSpec((tq, HEAD_DIM), lambda h, i: (i, q_off + h)),
                  pl.BlockSpec((s, HEAD_DIM), lambda h, i: (0, h)),
                  pl.BlockSpec((s, HEAD_DIM), lambda h, i: (0, h)),
                  pl.BlockSpec((None, 1, s), lambda h, i: (h, 0, 0))],
        out_specs=pl.BlockSpec((tq, HEAD_DIM), lambda h, i: (i, h)),
        out_shape=jax.ShapeDtypeStruct((s, BRANCH_W), BF16),
        scratch_shapes=[pltpu.VMEM((1, 1), F32)],
        compiler_params=pltpu.CompilerParams(dimension_semantics=("arbitrary", "arbitrary"),
                                             vmem_limit_bytes=_vmem(nbytes)),
    )(qq, k, v, neg_c.reshape(N_HEADS, 1, s))


def _pages_heads(refs):
    per_head = [jnp.concatenate([r[pl.ds(h, PAGE_SIZE, stride=N_HEADS), :] for r in refs], axis=0)
                for h in range(N_HEADS)]
    return jnp.stack(per_head).astype(BF16)


def _scores(q, kb):
    return jnp.einsum('htd,hsd->hts', q, kb, preferred_element_type=F32)


def _weighted(p, vb):
    return jnp.einsum('hts,hsd->htd', p.astype(BF16), vb, preferred_element_type=F32)


def _lane_blocks(x, width):
    return [x[:, g * width:(g + 1) * width] for g in range(x.shape[1] // width)]


def _suffix_within_pages(blocks, u):
    rows = blocks[0].shape[0]
    local = _split_dot(jnp.concatenate(blocks, axis=0), u)
    return [local[g * rows:(g + 1) * rows] for g in range(len(blocks))]


def _new_token_mask(inclusive, n_new):
    t = lax.broadcasted_iota(jnp.int32, (N_HEADS, T_PAD, PAGE_SIZE), 1)
    s = lax.broadcasted_iota(jnp.int32, (N_HEADS, T_PAD, PAGE_SIZE), 2)
    return jnp.logical_and((s <= t) if inclusive else (s < t), s < n_new)


def _sb_block(q, kb, vb, carry, acc, mask, u):
    z = _scores(q, kb)
    ls = _log_sigmoid(z)
    lk = ls - z
    if mask is not None:
        lk = jnp.where(mask, lk, 0.0)
    blocks = _lane_blocks(lk.reshape(N_HEADS * T_PAD, -1), PAGE_SIZE)
    local = _suffix_within_pages(blocks, u)
    parts = [None] * len(blocks)
    for g in reversed(range(len(blocks))):
        parts[g] = local[g] + carry
        carry = carry + jnp.sum(blocks[g], axis=1, keepdims=True)
    p = jnp.exp(ls + jnp.concatenate(parts, axis=1).reshape(ls.shape))
    if mask is not None:
        p = jnp.where(mask, p, 0.0)
    return carry, acc + _weighted(p, vb)


def _sb_sample_kernel(pt_ref, q_ref, kn_ref, vn_ref, *rest, n_new):
    kp = rest[:PAGES_PER_STEP]
    vp = rest[PAGES_PER_STEP:2 * PAGES_PER_STEP]
    o_ref, carry_ref, acc_ref = rest[2 * PAGES_PER_STEP:]
    j = pl.program_id(1)
    q = q_ref[...]
    u = _suffix_matrix(PAGE_SIZE)

    @pl.when(j == 0)
    def _():
        carry, acc = _sb_block(q, kn_ref[...], vn_ref[...], jnp.zeros((N_HEADS * T_PAD, 1), F32),
                               jnp.zeros((N_HEADS, T_PAD, HEAD_DIM), F32),
                               _new_token_mask(False, n_new), u)
        carry_ref[...] = carry
        acc_ref[...] = acc

    @pl.when(jnp.max(carry_ref[...]) > -EXP_CUTOFF)
    def _():
        carry, acc = _sb_block(q, _pages_heads(kp), _pages_heads(vp), carry_ref[...], acc_ref[...], None, u)
        carry_ref[...] = carry
        acc_ref[...] = acc

    @pl.when(j == pl.num_programs(1) - 1)
    def _():
        o_ref[...] = acc_ref[...].astype(o_ref.dtype)


def _fox_block(q, kb, vb, lf_pages, carry, m, l, acc, mask, u):
    local = _suffix_within_pages(lf_pages, u)
    later = jnp.zeros((N_HEADS, 1), F32)
    parts = [None] * len(lf_pages)
    for g in reversed(range(len(lf_pages))):
        parts[g] = local[g] + later
        later = later + jnp.sum(lf_pages[g], axis=1, keepdims=True)
    suffix = jnp.concatenate(parts, axis=1)
    logits = _scores(q, kb) + (carry + suffix[:, None, :])
    if mask is not None:
        logits = jnp.where(mask, logits, -jnp.inf)
    m_new = jnp.maximum(m, jnp.max(logits, axis=-1, keepdims=True))
    alpha = jnp.exp(m - m_new)
    p = jnp.exp(logits - m_new)
    l = alpha * l + jnp.sum(p, axis=-1, keepdims=True)
    acc = alpha * acc + _weighted(p, vb)
    return carry + later[:, None, :], m_new, l, acc


def _fox_sample_kernel(pt_ref, q_ref, kn_ref, vn_ref, lfn_ref, *rest, n_new):
    kp = rest[:PAGES_PER_STEP]
    vp = rest[PAGES_PER_STEP:2 * PAGES_PER_STEP]
    lfp = rest[2 * PAGES_PER_STEP:3 * PAGES_PER_STEP]
    o_ref, carry_ref, m_ref, l_ref, acc_ref = rest[3 * PAGES_PER_STEP:]
    j = pl.program_id(1)
    q = q_ref[...]
    u = _suffix_matrix(PAGE_SIZE)

    @pl.when(j == 0)
    def _():
        t = lax.broadcasted_iota(jnp.int32, (N_HEADS, T_PAD, PAGE_SIZE), 1)
        s = lax.broadcasted_iota(jnp.int32, (N_HEADS, T_PAD, PAGE_SIZE), 2)
        lfn = lfn_ref[...]
        suffix_new = _split_dot(lfn, u)
        carry0 = -jnp.sum(jnp.where(s == t, suffix_new[:, None, :], 0.0), axis=-1, keepdims=True)
        carry, m, l, acc = _fox_block(q, kn_ref[...], vn_ref[...], [lfn], carry0,
                                      jnp.full((N_HEADS, T_PAD, 1), -jnp.inf, F32),
                                      jnp.zeros((N_HEADS, T_PAD, 1), F32),
                                      jnp.zeros((N_HEADS, T_PAD, HEAD_DIM), F32),
                                      _new_token_mask(True, n_new), u)
        carry_ref[...] = carry
        m_ref[...] = m
        l_ref[...] = l
        acc_ref[...] = acc

    carry, m, l, acc = _fox_block(q, _pages_heads(kp), _pages_heads(vp), [r[...] for r in lfp],
                                  carry_ref[...], m_ref[...], l_ref[...], acc_ref[...], None, u)
    carry_ref[...] = carry
    m_ref[...] = m
    l_ref[...] = l
    acc_ref[...] = acc

    @pl.when(j == pl.num_programs(1) - 1)
    def _():
        o_ref[...] = (acc / l).astype(o_ref.dtype)


def _sample_attention(page_table, q, k_new, v_new, k_cache, v_cache, layer, *, n_new,
                      lf_new=None, lf_cache=None):
    n_req, n_pages = page_table.shape
    steps = n_pages // PAGES_PER_STEP
    fox = lf_cache is not None

    def page_map(g):
        def index(b, j, pt):
            return (layer, pt[b, n_pages - (j + 1) * PAGES_PER_STEP + g], 0, 0)
        return index

    small = lambda b, j, pt: (b, 0, 0, 0)
    in_specs = [pl.BlockSpec((None, N_HEADS, T_PAD, HEAD_DIM), small),
                pl.BlockSpec((None, N_HEADS, PAGE_SIZE, HEAD_DIM), small),
                pl.BlockSpec((None, N_HEADS, PAGE_SIZE, HEAD_DIM), small)]
    args = [q, k_new, v_new]
    if fox:
        in_specs.append(pl.BlockSpec((None, N_HEADS, PAGE_SIZE), lambda b, j, pt: (b, 0, 0)))
        args.append(lf_new)
    page_rows = PAGE_SIZE * N_HEADS
    for cache in (k_cache, v_cache):
        for g in range(PAGES_PER_STEP):
            in_specs.append(pl.BlockSpec((None, None, page_rows, HEAD_DIM), page_map(g)))
            args.append(cache)
    if fox:
        for g in range(PAGES_PER_STEP):
            in_specs.append(pl.BlockSpec((None, None, N_HEADS, PAGE_SIZE), page_map(g)))
            args.append(lf_cache)
    if fox:
        scratch = [pltpu.VMEM((N_HEADS, T_PAD, 1), F32)] * 3
    else:
        scratch = [pltpu.VMEM((N_HEADS * T_PAD, 1), F32)]
    scratch.append(pltpu.VMEM((N_HEADS, T_PAD, HEAD_DIM), F32))
    nbytes = 3 * PAGES_PER_STEP * _nbytes((page_rows, HEAD_DIM), F32)
    kern = functools.partial(_fox_sample_kernel if fox else _sb_sample_kernel, n_new=n_new)
    return pl.pallas_call(
        kern,
        grid_spec=pltpu.PrefetchScalarGridSpec(
            num_scalar_prefetch=1,
            grid=(n_req, steps),
            in_specs=in_specs,
            out_specs=pl.BlockSpec((None, N_HEADS, T_PAD, HEAD_DIM), small),
            scratch_shapes=scratch),
        out_shape=jax.ShapeDtypeStruct((n_req, N_HEADS, T_PAD, HEAD_DIM), BF16),
        compiler_params=pltpu.CompilerParams(dimension_semantics=("parallel", "arbitrary"),
                                             vmem_limit_bytes=_vmem(nbytes)),
    )(page_table, *args)


def _softmax_rows(s):
    m = jnp.max(s, axis=-1, keepdims=True)
    p = jnp.exp(s - m)
    return p, jnp.sum(p, axis=-1, keepdims=True)


def _mem_prompt_kernel(q_ref, k_ref, v_ref, o_ref):
    for h in range(MEM_HEADS):
        cols = slice(h * HEAD_DIM, (h + 1) * HEAD_DIM)
        s = lax.dot_general(q_ref[:, cols], k_ref[:, cols], (((1,), (1,)), ((), ())),
                            preferred_element_type=F32)
        p, l = _softmax_rows(s)
        o = jnp.dot(p.astype(BF16), v_ref[:, cols], preferred_element_type=F32)
        o_ref[:, cols] = (o / l).astype(o_ref.dtype)


def _mem_prompt(q, mk, mv, *, tm):
    m = q.shape[0]
    nbytes = 2 * _nbytes((tm, MEM_W), BF16) + 4 * _nbytes((tm, MEM_TOKENS), F32)
    return pl.pallas_call(
        _mem_prompt_kernel,
        grid=(m // tm,),
        in_specs=[pl.BlockSpec((tm, MEM_W), lambda i: (i, 0)),
                  pl.BlockSpec((MEM_TOKENS, MEM_W), lambda i: (0, 0)),
                  pl.BlockSpec((MEM_TOKENS, MEM_W), lambda i: (0, 0))],
        out_specs=pl.BlockSpec((tm, MEM_W), lambda i: (i, 0)),
        out_shape=jax.ShapeDtypeStruct((m, MEM_W), BF16),
        compiler_params=pltpu.CompilerParams(dimension_semantics=("parallel",),
                                             vmem_limit_bytes=_vmem(nbytes)),
    )(q, mk, mv)


def _mem_sample_kernel(q_ref, k_ref, v_ref, o_ref):
    for h in range(MEM_HEADS):
        cols = slice(h * HEAD_DIM, (h + 1) * HEAD_DIM)
        kh = k_ref[pl.ds(h, MEM_TOKENS, stride=MEM_HEADS), :].astype(BF16)
        vh = v_ref[pl.ds(h, MEM_TOKENS, stride=MEM_HEADS), :].astype(BF16)
        s = lax.dot_general(q_ref[:, cols], kh, (((1,), (1,)), ((), ())), preferred_element_type=F32)
        p, l = _softmax_rows(s)
        o = jnp.dot(p.astype(BF16), vh, preferred_element_type=F32)
        o_ref[:, cols] = (o / l).astype(o_ref.dtype)


def _mem_sample(q, k_cache, v_cache, layer):
    n_req = q.shape[0]
    rows = MEM_TOKENS * MEM_HEADS
    cache_spec = pl.BlockSpec((None, None, rows, HEAD_DIM), lambda b: (layer, b, 0, 0))
    return pl.pallas_call(
        _mem_sample_kernel,
        grid=(n_req,),
        in_specs=[pl.BlockSpec((None, T_PAD, MEM_W), lambda b: (b, 0, 0)), cache_spec, cache_spec],
        out_specs=pl.BlockSpec((None, T_PAD, MEM_W), lambda b: (b, 0, 0)),
        out_shape=jax.ShapeDtypeStruct((n_req, T_PAD, MEM_W), BF16),
        compiler_params=pltpu.CompilerParams(dimension_semantics=("parallel",)),
    )(q, k_cache, v_cache)


def _layer_weights(l, w_in, b_forget, gm_ln_g, gm_ln_b, gm_w_s, gm_b_s, w_br_gm, w_br_sb, w_br_fox,
                   w_mix_out, ln1_g, ln1_b, mem_w_q, mem_w_k, mem_w_v, mem_w_out, ln2_g, ln2_b,
                   ffn_w_gate, ffn_w_up, ffn_w_down, ln3_g, ln3_b):
    o_fg = 8 * BRANCH_W
    o_gate = o_fg + N_HEADS
    row = lambda a: a[l].reshape(1, -1).astype(F32)
    stacked = lambda a: (a.astype(BF16), l, False)
    w_in_t = jnp.swapaxes(w_in, 1, 2).astype(BF16)
    pad_fg = jnp.zeros((DEPTH, LANES - N_HEADS, D_MODEL), BF16)
    return dict(
        w_main=(w_in_t, l, True),
        w_fg=(jnp.concatenate([w_in_t[:, o_fg:o_gate], pad_fg], axis=1), l, True),
        b_fg=jnp.concatenate([b_forget[l].astype(F32), jnp.zeros((LANES - N_HEADS,), F32)]).reshape(1, LANES),
        w_gate=(w_in_t[:, o_gate:], l, True),
        gm_ln_g=row(gm_ln_g), gm_ln_b=row(gm_ln_b),
        gm_w_s=gm_w_s[l].astype(F32), gm_b_s=gm_b_s[l].astype(F32),
        w_br_gm=stacked(w_br_gm), w_br_sb=stacked(w_br_sb),
        w_br_fox=stacked(w_br_fox), w_mix_out=stacked(w_mix_out),
        ln1_g=row(ln1_g), ln1_b=row(ln1_b),
        mem_w_q=stacked(mem_w_q), mem_w_k=stacked(mem_w_k),
        mem_w_v=stacked(mem_w_v), mem_w_out=stacked(mem_w_out),
        ln2_g=row(ln2_g), ln2_b=row(ln2_b),
        ffn_w_gate=stacked(ffn_w_gate), ffn_w_up=stacked(ffn_w_up), ffn_w_down=stacked(ffn_w_down),
        ln3_g=row(ln3_g), ln3_b=row(ln3_b),
    )


def _pair(x, handle, col_block=None):
    w, layer, transposed = handle
    return (x, w, layer, col_block, transposed)


def _in_proj(x_bf, p, kv_stacked, *, tm):
    tn = 512
    seg = BRANCH_W // tn
    main = p['w_main']
    u, = _mm([_pair(x_bf, main)], _epi_gelu, [BF16], tm=tm, tn=tn, n_cols=BRANCH_W, sub_cols=SUB)
    v, = _mm([_pair(x_bf, main, lambda j: j + 1)], _epi_gelu_ln, [F32], tm=min(tm, 512), tn=BRANCH_W,
             n_cols=BRANCH_W, sub_rows=SUB, extras=[('row', p['gm_ln_g']), ('row', p['gm_ln_b'])])
    qq, = _mm([_pair(x_bf, main, lambda j: jnp.where(j < seg, j + 2 * seg, j + 4 * seg))],
              _epi_scale, [BF16], tm=tm, tn=tn, n_cols=2 * BRANCH_W, sub_cols=SUB)
    kv_stacked = kv_stacked or [None] * 4
    kv = [_kv_proj(x_bf, main, segment, prev, tm=tm)
          for segment, prev in zip((3, 4, 6, 7), kv_stacked)]
    logf, = _mm([_pair(x_bf, p['w_fg'])], _epi_logsig, [F32], tm=tm, tn=LANES, extras=[('row', p['b_fg'])])
    gates, = _mm([_pair(x_bf, p['w_gate'])], _epi_sigmoid, [BF16], tm=tm, tn=tn, sub_cols=SUB)
    return u, v, qq, [a for a, _ in kv], [b for _, b in kv], logf, gates


def _post_mixers(x, y_gm, y_sb, y_fx, gates, p, *, tm):
    gate_blocks = D_MODEL // 512
    h, = _mm([_pair(y_gm, p['w_br_gm']), _pair(y_sb, p['w_br_sb']), _pair(y_fx, p['w_br_fox'])],
             _epi_merge, [BF16], tm=min(tm, 512), tn=512, sub_cols=SUB,
             extras=[('tile', gates, 0), ('tile', gates, gate_blocks), ('tile', gates, 2 * gate_blocks)])
    return _mm([_pair(h, p['w_mix_out'])], _epi_resid_ln, [F32, BF16], tm=min(tm, 512), tn=D_MODEL,
               sub_rows=SUB, extras=[('tile', x, 0), ('row', p['ln1_g']), ('row', p['ln1_b'])])


def _mem_query(x1_bf, p, *, tm):
    q, = _mm([_pair(x1_bf, p['mem_w_q'])], _epi_scale, [BF16], tm=tm, tn=MEM_W)
    return q


def _post_mem(x1, o, p, *, tm):
    return _mm([_pair(o, p['mem_w_out'])], _epi_resid_ln, [F32, BF16], tm=min(tm, 512), tn=D_MODEL,
               sub_rows=SUB, extras=[('tile', x1, 0), ('row', p['ln2_g']), ('row', p['ln2_b'])])


def _ffn(x2, x2_bf, p, *, tm):
    hid, = _mm([_pair(x2_bf, p['ffn_w_gate']), _pair(x2_bf, p['ffn_w_up'])], _epi_swiglu, [BF16],
               tm=tm, tn=512, sub_cols=SUB)
    return _mm([_pair(hid, p['ffn_w_down'])], _epi_resid_ln, [F32, BF16], tm=min(tm, 256), tn=D_MODEL,
               sub_rows=SUB // 2, extras=[('tile', x2, 0), ('row', p['ln3_g']), ('row', p['ln3_b'])])


def _heads_first(a, n_req, t, pad_to, dtype):
    a = a.reshape(n_req, t, N_HEADS, HEAD_DIM).transpose(0, 2, 1, 3).astype(dtype)
    return jnp.pad(a, ((0, 0), (0, 0), (0, pad_to - t), (0, 0)))


def kernel(x_prompt, x_sample, cache_sb_k, cache_sb_v, cache_fox_k, cache_fox_v, cache_fox_logf,
           cache_mem_k, cache_mem_v, page_table, mem_prompt, w_in, b_forget, gm_ln_g, gm_ln_b,
           gm_w_s, gm_b_s, w_br_gm, w_br_sb, w_br_fox, w_mix_out, ln1_g, ln1_b, mem_w_q, mem_w_k,
           mem_w_v, mem_w_out, ln2_g, ln2_b, ffn_w_gate, ffn_w_up, ffn_w_down, ln3_g, ln3_b):
    batch, seq, _ = x_prompt.shape
    n_req, t_new, _ = x_sample.shape
    assert batch == 1 and t_new <= T_PAD
    n_pool = cache_sb_k.shape[1]
    rows_s = n_req * t_new
    bw = BRANCH_W

    page_rows = PAGE_SIZE * N_HEADS
    pool_view = lambda c: c.reshape(DEPTH, n_pool, page_rows, HEAD_DIM)
    c_sb_k, c_sb_v, c_fx_k, c_fx_v = map(pool_view, (cache_sb_k, cache_sb_v, cache_fox_k, cache_fox_v))
    c_fx_lf = cache_fox_logf.astype(F32).transpose(0, 1, 3, 2)
    mem_view = lambda c: c.reshape(DEPTH, n_req, MEM_TOKENS * MEM_HEADS, HEAD_DIM)
    c_mem_k, c_mem_v = mem_view(cache_mem_k), mem_view(cache_mem_v)
    mem_bf = mem_prompt.reshape(MEM_TOKENS, D_MODEL).astype(BF16)

    xp = x_prompt.reshape(seq, D_MODEL)
    xs = x_sample.reshape(rows_s, D_MODEL)
    xp_bf, xs_bf = xp.astype(BF16), xs.astype(BF16)
    outs = {k: [] for k in ('fxf_p', 'mk_p', 'mv_p', 'gmv_s', 'fxf_s')}
    kv_p = kv_s = None
    tm_p, tm_s, tq, tk = 1024, rows_s, 512, 512
    for l in range(DEPTH):
        p = _layer_weights(l, w_in, b_forget, gm_ln_g, gm_ln_b, gm_w_s, gm_b_s, w_br_gm, w_br_sb,
                           w_br_fox, w_mix_out, ln1_g, ln1_b, mem_w_q, mem_w_k, mem_w_v, mem_w_out,
                           ln2_g, ln2_b, ffn_w_gate, ffn_w_up, ffn_w_down, ln3_g, ln3_b)

        u, v, qq, kv_p, kv_bf, logf, gates = _in_proj(xp_bf, p, kv_p, tm=tm_p)
        y_gm = _gmlp(u, v, p['gm_w_s'], p['gm_b_s'].reshape(N_HEADS, GM_CHUNK, 1), tm=tm_p)
        y_sb = _sb_prompt(qq, kv_bf[0], kv_bf[1], q_off=0, tq=tq, tk=tk, sub=256)
        y_fx = _fox_prompt(qq, kv_bf[2], kv_bf[3], _neg_cumsum(logf), q_off=N_HEADS, tq=tq, tk=tk)
        x1, x1_bf = _post_mixers(xp, y_gm, y_sb, y_fx, gates, p, tm=tm_p)
        mk, mk_bf = _mm([_pair(mem_bf, p['mem_w_k'])], _epi_dual, [F32, BF16], tm=MEM_TOKENS, tn=MEM_W)
        mv, mv_bf = _mm([_pair(mem_bf, p['mem_w_v'])], _epi_dual, [F32, BF16], tm=MEM_TOKENS, tn=MEM_W)
        o_mem = _mem_prompt(_mem_query(x1_bf, p, tm=tm_p), mk_bf, mv_bf, tm=512)
        x2, x2_bf = _post_mem(x1, o_mem, p, tm=tm_p)
        xp, xp_bf = _ffn(x2, x2_bf, p, tm=tm_p)
        outs['fxf_p'].append(logf[:, :N_HEADS].reshape(batch, seq, N_HEADS))
        outs['mk_p'].append(mk.reshape(batch, MEM_TOKENS, MEM_HEADS, HEAD_DIM))
        outs['mv_p'].append(mv.reshape(batch, MEM_TOKENS, MEM_HEADS, HEAD_DIM))

        u, v, qq, kv_s, kv_bf, logf, gates = _in_proj(xs_bf, p, kv_s, tm=tm_s)
        ws_s = jnp.tile(p['gm_w_s'][:, :t_new, :t_new], (1, GM_CHUNK // t_new, GM_CHUNK // t_new))
        bs_s = jnp.tile(p['gm_b_s'][:, :t_new], (1, GM_CHUNK // t_new)).reshape(N_HEADS, GM_CHUNK, 1)
        y_gm = _gmlp(u, v, ws_s, bs_s, tm=tm_s, rows_per_request=t_new)
        hf = functools.partial(_heads_first, n_req=n_req, t=t_new, dtype=BF16)
        q_sb, q_fx = hf(qq[:, :bw], pad_to=T_PAD), hf(qq[:, bw:], pad_to=T_PAD)
        kn_sb, vn_sb, kn_fx, vn_fx = (hf(a, pad_to=PAGE_SIZE) for a in kv_bf)
        lf_s = logf[:, :N_HEADS].reshape(n_req, t_new, N_HEADS)
        lf_new = jnp.pad(lf_s.transpose(0, 2, 1), ((0, 0), (0, 0), (0, PAGE_SIZE - t_new)))
        o_sb = _sample_attention(page_table, q_sb, kn_sb, vn_sb, c_sb_k, c_sb_v, l, n_new=t_new)
        o_fx = _sample_attention(page_table, q_fx, kn_fx, vn_fx, c_fx_k, c_fx_v, l, n_new=t_new,
                                 lf_new=lf_new, lf_cache=c_fx_lf)
        rows = lambda o: o[:, :, :t_new].transpose(0, 2, 1, 3).reshape(rows_s, bw)
        x1, x1_bf = _post_mixers(xs, y_gm, rows(o_sb), rows(o_fx), gates, p, tm=tm_s)
        q_mem = _mem_query(x1_bf, p, tm=tm_s).reshape(n_req, t_new, MEM_W)
        q_mem = jnp.pad(q_mem, ((0, 0), (0, T_PAD - t_new), (0, 0)))
        o_mem = _mem_sample(q_mem, c_mem_k, c_mem_v, l)[:, :t_new].reshape(rows_s, MEM_W)
        x2, x2_bf = _post_mem(x1, o_mem, p, tm=tm_s)
        xs, xs_bf = _ffn(x2, x2_bf, p, tm=tm_s)
        outs['gmv_s'].append(v.reshape(n_req, t_new, bw))
        outs['fxf_s'].append(lf_s)

    st = jnp.stack
    kv_p = [a.reshape(DEPTH, batch, seq, N_HEADS, HEAD_DIM) for a in kv_p]
    kv_s = [a.reshape(DEPTH, n_req, t_new, N_HEADS, HEAD_DIM) for a in kv_s]
    return (xp.reshape(batch, seq, D_MODEL), xs.reshape(n_req, t_new, D_MODEL),
            *kv_p, st(outs['fxf_p']), st(outs['mk_p']), st(outs['mv_p']), st(outs['gmv_s']),
            *kv_s, st(outs['fxf_s']))
```

```python
import functools

import jax
import jax.numpy as jnp
from jax import lax
from jax.experimental import pallas as pl
from jax.experimental.pallas import tpu as pltpu

F32 = jnp.float32
BF16 = jnp.bfloat16

D_MODEL = 2048
DEPTH = 2
HEAD_DIM = 128
N_HEADS = 8
BRANCH_W = N_HEADS * HEAD_DIM
GM_CHUNK = 128
MEM_TOKENS = 256
MEM_HEADS = 4
MEM_W = MEM_HEADS * HEAD_DIM
PAGE_SIZE = 128
ALPHA = (2.0 * DEPTH) ** 0.25
LN_EPS = 1e-5
ATTN_SCALE = HEAD_DIM ** -0.5

LANES = 128
SUBLANES = 8
VMEM_CAP = 56 << 20
PAGES_PER_STEP = 16
T_PAD = SUBLANES
EXP_CUTOFF = 110.0
SUB = 256


def _vmem(nbytes):
    return int(min(VMEM_CAP, max(16 << 20, 2 * nbytes + (8 << 20))))


def _nbytes(shape, dtype):
    n = 1
    for s in shape:
        n *= s
    return n * jnp.dtype(dtype).itemsize


def _sigmoid(x):
    return 1.0 / (1.0 + jnp.exp(-x))


def _neg_abs(x):
    bits = lax.bitcast_convert_type(x, jnp.int32) | jnp.int32(-2 ** 31)
    return lax.bitcast_convert_type(bits, F32)


def _log_sigmoid(x):
    return jnp.minimum(x, 0.0) - jnp.log(1.0 + jnp.exp(_neg_abs(x)))


def _layernorm(y, g, b):
    mu = jnp.mean(y, axis=-1, keepdims=True)
    d = y - mu
    var = jnp.mean(d * d, axis=-1, keepdims=True)
    return d * lax.rsqrt(var + LN_EPS) * g + b


def _split_dot(x, m_bf16):
    hi = x.astype(BF16)
    lo = (x - hi.astype(F32)).astype(BF16)
    return (jnp.dot(hi, m_bf16, preferred_element_type=F32)
            + jnp.dot(lo, m_bf16, preferred_element_type=F32))


def _dot_nt(x, w_t):
    return lax.dot_general(x, w_t, (((1,), (1,)), ((), ())), preferred_element_type=F32)


def _mm_kernel(*refs, transposed, extra_kinds, n_out, epilogue, sub_rows, sub_cols):
    n_pairs = len(transposed)
    pairs = refs[:2 * n_pairs]
    extra = refs[2 * n_pairs:2 * n_pairs + len(extra_kinds)]
    outs = refs[2 * n_pairs + len(extra_kinds):2 * n_pairs + len(extra_kinds) + n_out]
    tm, tn = outs[0].shape
    for r0 in range(0, tm, sub_rows):
        rows = slice(r0, r0 + sub_rows)
        for c0 in range(0, tn, sub_cols):
            cols = slice(c0, c0 + sub_cols)
            accs = [_dot_nt(pairs[2 * p][rows, :], pairs[2 * p + 1][cols, :]) if transposed[p] else
                    jnp.dot(pairs[2 * p][rows, :], pairs[2 * p + 1][:, cols], preferred_element_type=F32)
                    for p in range(n_pairs)]
            ex = [e[:, cols] if kind == 'row' else e[rows, cols] for e, kind in zip(extra, extra_kinds)]
            for o, r in zip(outs, epilogue(*accs, *ex)):
                o[rows, cols] = r.astype(o.dtype)


def _mm(pairs, epilogue, out_dtypes, *, tm, tn, extras=(), n_cols=None, sub_rows=None, sub_cols=None):
    m = pairs[0][0].shape[0]
    n = n_cols if n_cols is not None else pairs[0][1].shape[1 if pairs[0][4] else 2]
    tm = min(tm, m)
    tn = min(tn, n)
    sub_rows = min(sub_rows or tm, tm)
    sub_cols = min(sub_cols or tn, tn)
    assert m % tm == 0 and n % tn == 0 and tm % sub_rows == 0 and tn % sub_cols == 0
    w_mode = dict(pipeline_mode=pl.Buffered(1)) if tn == n and n_cols is None else {}
    in_specs, args, nbytes = [], [], 0
    for x, w, layer, col_block, transposed in pairs:
        kk = x.shape[1]
        col_block = col_block or (lambda j: j)
        in_specs.append(pl.BlockSpec((tm, kk), lambda i, j: (i, 0)))
        if transposed:
            in_specs.append(pl.BlockSpec((None, tn, kk),
                                         lambda i, j, layer=layer, cb=col_block: (layer, cb(j), 0), **w_mode))
        else:
            in_specs.append(pl.BlockSpec((None, kk, tn),
                                         lambda i, j, layer=layer, cb=col_block: (layer, 0, cb(j)), **w_mode))
        args += [x, w]
        nbytes += _nbytes((tm, kk), x.dtype) + _nbytes((kk, tn), w.dtype) // (2 if w_mode else 1)
    for e in extras:
        if e[0] == 'row':
            in_specs.append(pl.BlockSpec((1, tn), lambda i, j: (0, j)))
            nbytes += _nbytes((SUBLANES, tn), e[1].dtype)
        else:
            off = e[2]
            in_specs.append(pl.BlockSpec((tm, tn), lambda i, j, off=off: (i, j + off)))
            nbytes += _nbytes((tm, tn), e[1].dtype)
        args.append(e[1])
    out_shape = [jax.ShapeDtypeStruct((m, n), dt) for dt in out_dtypes]
    out_specs = [pl.BlockSpec((tm, tn), lambda i, j: (i, j)) for _ in out_dtypes]
    nbytes += sum(_nbytes((tm, tn), dt) for dt in out_dtypes)
    nbytes += (len(pairs) + 2) * _nbytes((sub_rows, sub_cols), F32)
    kern = functools.partial(_mm_kernel, transposed=tuple(bool(pr[4]) for pr in pairs),
                             extra_kinds=tuple(e[0] for e in extras),
                             n_out=len(out_dtypes), epilogue=epilogue, sub_rows=sub_rows, sub_cols=sub_cols)
    return pl.pallas_call(
        kern,
        grid=(m // tm, n // tn),
        in_specs=in_specs,
        out_specs=out_specs,
        out_shape=out_shape,
        compiler_params=pltpu.CompilerParams(
            dimension_semantics=("parallel", "parallel"),
            vmem_limit_bytes=_vmem(nbytes)),
    )(*args)


def _kv_kernel(*refs):
    x_ref, w_ref = refs[:2]
    o32_ref, obf_ref = refs[-2:]
    acc = _dot_nt(x_ref[...], w_ref[...])
    obf_ref[...] = acc.astype(obf_ref.dtype)
    for h in range(N_HEADS):
        o32_ref[pl.ds(h, acc.shape[0], stride=N_HEADS), :] = acc[:, h * HEAD_DIM:(h + 1) * HEAD_DIM]


def _kv_proj(x_bf, handle, segment, stacked_out, *, tm):
    w, layer, transposed = handle
    assert transposed
    m, kdim = x_bf.shape
    tm = min(tm, m)
    in_specs = [pl.BlockSpec((tm, kdim), lambda i: (i, 0)),
                pl.BlockSpec((None, BRANCH_W, kdim), lambda i: (layer, segment, 0))]
    args = [x_bf, w]
    aliases = {}
    if stacked_out is not None:
        in_specs.append(pl.BlockSpec(memory_space=pl.ANY))
        args.append(stacked_out)
        aliases = {2: 0}
    nbytes = (_nbytes((tm, kdim), BF16) + _nbytes((kdim, BRANCH_W), BF16)
              + 2 * _nbytes((tm, BRANCH_W), F32) + _nbytes((tm, BRANCH_W), BF16))
    return pl.pallas_call(
        _kv_kernel,
        grid=(m // tm,),
        in_specs=in_specs,
        out_specs=[pl.BlockSpec((None, tm * N_HEADS, HEAD_DIM), lambda i: (layer, i, 0)),
                   pl.BlockSpec((tm, BRANCH_W), lambda i: (i, 0))],
        out_shape=[jax.ShapeDtypeStruct((DEPTH, m * N_HEADS, HEAD_DIM), F32),
                   jax.ShapeDtypeStruct((m, BRANCH_W), BF16)],
        input_output_aliases=aliases,
        compiler_params=pltpu.CompilerParams(dimension_semantics=("parallel",),
                                             vmem_limit_bytes=_vmem(nbytes)),
    )(*args)


def _epi_gelu(acc):
    return (jax.nn.gelu(acc),)


def _epi_gelu_ln(acc, g, b):
    return (_layernorm(jax.nn.gelu(acc), g, b),)


def _epi_scale(acc):
    return (acc * ATTN_SCALE,)


def _epi_dual(acc):
    return (acc, acc)


def _epi_logsig(acc, b):
    return (_log_sigmoid(acc + b),)


def _epi_sigmoid(acc):
    return (_sigmoid(acc),)


def _epi_merge(a_gm, a_sb, a_fx, g_gm, g_sb, g_fx):
    return (g_gm.astype(F32) * a_gm + g_sb.astype(F32) * a_sb + g_fx.astype(F32) * a_fx,)


def _epi_resid_ln(acc, resid, g, b):
    y = _layernorm(ALPHA * resid + acc, g, b)
    return (y, y)


def _epi_swiglu(a_gate, a_up):
    return (a_gate * _sigmoid(a_gate) * a_up,)


def _gmlp_kernel(u_ref, v_ref, ws_ref, bs_ref, o_ref, *, n_chunks, rows_per_request):
    r = lax.broadcasted_iota(jnp.int32, (GM_CHUNK, GM_CHUNK), 0)
    c = lax.broadcasted_iota(jnp.int32, (GM_CHUNK, GM_CHUNK), 1)
    mask = c <= r
    if rows_per_request is not None:
        mask = jnp.logical_and(mask, (r // rows_per_request) == (c // rows_per_request))
    for g in range(N_HEADS):
        w = jnp.where(mask, ws_ref[g], 0.0).astype(BF16)
        b = bs_ref[g]
        cols = slice(g * LANES, (g + 1) * LANES)
        for n in range(n_chunks):
            rows = slice(n * GM_CHUNK, (n + 1) * GM_CHUNK)
            mixed = jnp.dot(w, v_ref[rows, cols].astype(BF16), preferred_element_type=F32) + b
            o_ref[rows, cols] = (u_ref[rows, cols].astype(F32) * mixed).astype(o_ref.dtype)


def _gmlp(u, v, ws, bs, *, tm, rows_per_request=None):
    m = u.shape[0]
    tm = min(tm, m)
    nbytes = _nbytes((tm, BRANCH_W), BF16) * 2 + _nbytes((tm, BRANCH_W), F32)
    kern = functools.partial(_gmlp_kernel, n_chunks=tm // GM_CHUNK, rows_per_request=rows_per_request)
    return pl.pallas_call(
        kern,
        grid=(m // tm,),
        in_specs=[pl.BlockSpec((tm, BRANCH_W), lambda i: (i, 0)),
                  pl.BlockSpec((tm, BRANCH_W), lambda i: (i, 0)),
                  pl.BlockSpec((N_HEADS, GM_CHUNK, GM_CHUNK), lambda i: (0, 0, 0)),
                  pl.BlockSpec((N_HEADS, GM_CHUNK, 1), lambda i: (0, 0, 0))],
        out_specs=pl.BlockSpec((tm, BRANCH_W), lambda i: (i, 0)),
        out_shape=jax.ShapeDtypeStruct((m, BRANCH_W), BF16),
        compiler_params=pltpu.CompilerParams(dimension_semantics=("parallel",),
                                             vmem_limit_bytes=_vmem(nbytes)),
    )(u, v, ws, bs)


def _suffix_matrix(n):
    r = lax.broadcasted_iota(jnp.int32, (n, n), 0)
    c = lax.broadcasted_iota(jnp.int32, (n, n), 1)
    return jnp.where(r > c, 1.0, 0.0).astype(BF16)


def _causal_walk(step, init, live, i, tq, tk):
    groups = tq // tk
    base = i * groups
    states = []
    for g in range(groups):
        rows = slice(g * tk, (g + 1) * tk)
        st = step(rows, base + g, init(tk), True)
        for c in reversed(range(g)):
            st = step(rows, base + c, st, False)
        states.append(st)
    state = tuple(jnp.concatenate(parts, axis=0) for parts in zip(*states))
    full = slice(0, tq)

    def pending(c):
        n, st = c
        return jnp.logical_and(n < base, live(st, jnp.maximum(base - 1 - n, 0)))

    def one(c):
        n, st = c
        return n + 1, step(full, base - 1 - n, st, False)

    return lax.while_loop(pending, one, (jnp.int32(0), state))[1]


def _sb_prompt_kernel(q_ref, k_ref, v_ref, o_ref, *, tq, tk, sub):
    u = _suffix_matrix(sub)
    row = lax.broadcasted_iota(jnp.int32, (tk, tk), 0)
    col = lax.broadcasted_iota(jnp.int32, (tk, tk), 1)
    visible = col < row

    def step(rows, j, state, masked):
        carry, acc = state
        start = pl.multiple_of(j * tk, tk)
        z = lax.dot_general(q_ref[rows, :], k_ref[pl.ds(start, tk), :], (((1,), (1,)), ((), ())),
                            preferred_element_type=F32)
        ls = _log_sigmoid(z)
        lk = ls - z
        if masked:
            lk = jnp.where(visible, lk, 0.0)
        parts = [None] * (tk // sub)
        for c in reversed(range(tk // sub)):
            blk = lk[:, c * sub:(c + 1) * sub]
            parts[c] = jnp.dot(blk.astype(BF16), u, preferred_element_type=F32) + carry
            carry = carry + jnp.sum(blk, axis=1, keepdims=True)
        p = jnp.exp(ls + jnp.concatenate(parts, axis=1))
        if masked:
            p = jnp.where(visible, p, 0.0)
        acc = acc + jnp.dot(p.astype(BF16), v_ref[pl.ds(start, tk), :], preferred_element_type=F32)
        return carry, acc

    def live(state, j):
        return jnp.max(state[0]) > -EXP_CUTOFF

    init = lambda n: (jnp.zeros((n, 1), F32), jnp.zeros((n, HEAD_DIM), F32))
    _, acc = _causal_walk(step, init, live, pl.program_id(1), tq, tk)
    o_ref[...] = acc.astype(o_ref.dtype)


def _sb_prompt(qq, k, v, *, q_off, tq, tk, sub):
    s = qq.shape[0]
    tq, tk = min(tq, s), min(tk, s)
    nbytes = 2 * _nbytes((s, HEAD_DIM), BF16) + 8 * _nbytes((tq, tk), F32)
    return pl.pallas_call(
        functools.partial(_sb_prompt_kernel, tq=tq, tk=tk, sub=min(sub, tk)),
        grid=(N_HEADS, s // tq),
        in_specs=[pl.BlockSpec((tq, HEAD_DIM), lambda h, i: (i, q_off + h)),
                  pl.BlockSpec((s, HEAD_DIM), lambda h, i: (0, h)),
                  pl.BlockSpec((s, HEAD_DIM), lambda h, i: (0, h))],
        out_specs=pl.BlockSpec((tq, HEAD_DIM), lambda h, i: (i, h)),
        out_shape=jax.ShapeDtypeStruct((s, BRANCH_W), BF16),
        compiler_params=pltpu.CompilerParams(dimension_semantics=("parallel", "arbitrary"),
                                             vmem_limit_bytes=_vmem(nbytes)),
    )(qq, k, v)


def _neg_cumsum_kernel(lf_ref, o_ref, carry_ref):
    @pl.when(pl.program_id(0) == 0)
    def _():
        carry_ref[...] = jnp.zeros_like(carry_ref)

    n = LANES
    r = lax.broadcasted_iota(jnp.int32, (n, n), 0)
    c = lax.broadcasted_iota(jnp.int32, (n, n), 1)
    incl = jnp.where(r <= c, 1.0, 0.0).astype(F32)
    carry = carry_ref[...]
    for b in range(lf_ref.shape[0] // n):
        lft = lf_ref[b * n:(b + 1) * n, :].T
        cum = jnp.dot(lft, incl, preferred_element_type=F32, precision=lax.Precision.HIGHEST) + carry
        carry = cum[:, n - 1:n]
        o_ref[:, b * n:(b + 1) * n] = -cum[:N_HEADS, :]
    carry_ref[...] = carry


def _neg_cumsum(logf_pad):
    s = logf_pad.shape[0]
    n = min(s, 8 * LANES)
    return pl.pallas_call(
        _neg_cumsum_kernel,
        grid=(s // n,),
        in_specs=[pl.BlockSpec((n, LANES), lambda i: (i, 0))],
        out_specs=pl.BlockSpec((N_HEADS, n), lambda i: (0, i)),
        out_shape=jax.ShapeDtypeStruct((N_HEADS, s), F32),
        scratch_shapes=[pltpu.VMEM((LANES, 1), F32)],
        compiler_params=pltpu.CompilerParams(dimension_semantics=("arbitrary",)),
    )(logf_pad)


def _row_norms(x):
    x = x.astype(F32)
    return jnp.sqrt(jnp.sum(x * x, axis=1, keepdims=True))


def _fox_prompt_kernel(q_ref, k_ref, v_ref, nc_ref, o_ref, kmax_ref, *, tq, tk):
    row = lax.broadcasted_iota(jnp.int32, (tk, tk), 0)
    col = lax.broadcasted_iota(jnp.int32, (tk, tk), 1)
    visible = col <= row

    @pl.when(pl.program_id(1) == 0)
    def _():
        longest = jnp.zeros((tk, 1), F32)
        for c in range(k_ref.shape[0] // tk):
            longest = jnp.maximum(longest, _row_norms(k_ref[c * tk:(c + 1) * tk, :]))
        kmax_ref[...] = jnp.max(longest, axis=0, keepdims=True)

    score_bound = _row_norms(q_ref[...]) * kmax_ref[...]

    def step(rows, j, state, masked):
        m, l, acc = state
        start = pl.multiple_of(j * tk, tk)
        z = lax.dot_general(q_ref[rows, :], k_ref[pl.ds(start, tk), :], (((1,), (1,)), ((), ())),
                            preferred_element_type=F32)
        logits = z + nc_ref[:, pl.ds(start, tk)]
        if masked:
            logits = jnp.where(visible, logits, -jnp.inf)
        m_new = jnp.maximum(m, jnp.max(logits, axis=1, keepdims=True))
        alpha = jnp.exp(m - m_new)
        p = jnp.exp(logits - m_new)
        l = alpha * l + jnp.sum(p, axis=1, keepdims=True)
        acc = alpha * acc + jnp.dot(p.astype(BF16), v_ref[pl.ds(start, tk), :],
                                    preferred_element_type=F32)
        return m_new, l, acc

    def live(state, j):
        start = pl.multiple_of(j * tk, tk)
        newest = jnp.max(nc_ref[:, pl.ds(start, tk)])
        return jnp.max(score_bound - state[0]) + newest > -EXP_CUTOFF

    init = lambda n: (jnp.full((n, 1), -jnp.inf, F32), jnp.zeros((n, 1), F32),
                      jnp.zeros((n, HEAD_DIM), F32))
    _, l, acc = _causal_walk(step, init, live, pl.program_id(1), tq, tk)
    o_ref[...] = (acc / l).astype(o_ref.dtype)


def _fox_prompt(qq, k, v, neg_c, *, q_off, tq, tk):
    s = qq.shape[0]
    tq, tk = min(tq, s), min(tk, s)
    nbytes = 2 * _nbytes((s, HEAD_DIM), BF16) + 8 * _nbytes((tq, tk), F32)
    return pl.pallas_call(
        functools.partial(_fox_prompt_kernel, tq=tq, tk=tk),
        grid=(N_HEADS, s // tq),
        in_specs=[pl.BlockSpec((tq, HEAD_DIM), lambda h, i: (i, q_off + h)),
                  pl.BlockSpec((s, HEAD_DIM), lambda h, i: (0, h)),
                  pl.BlockSpec((s, HEAD_DIM), lambda h, i: (0, h)),
                  pl.BlockSpec((None, 1, s), lambda h, i: (h, 0, 0))],
        out_specs=pl.BlockSpec((tq, HEAD_DIM), lambda h, i: (i, h)),
        out_shape=jax.ShapeDtypeStruct((s, BRANCH_W), BF16),
        scratch_shapes=[pltpu.VMEM((1, 1), F32)],
        compiler_params=pltpu.CompilerParams(dimension_semantics=("arbitrary", "arbitrary"),
                                             vmem_limit_bytes=_vmem(nbytes)),
    )(qq, k, v, neg_c.reshape(N_HEADS, 1, s))


def _pages_heads(refs):
    per_head = [jnp.concatenate([r[pl.ds(h, PAGE_SIZE, stride=N_HEADS), :] for r in refs], axis=0)
                for h in range(N_HEADS)]
    return jnp.stack(per_head).astype(BF16)


def _scores(q, kb):
    return jnp.einsum('htd,hsd->hts', q, kb, preferred_element_type=F32)


def _weighted(p, vb):
    return jnp.einsum('hts,hsd->htd', p.astype(BF16), vb, preferred_element_type=F32)


def _lane_blocks(x, width):
    return [x[:, g * width:(g + 1) * width] for g in range(x.shape[1] // width)]


def _suffix_within_pages(blocks, u):
    rows = blocks[0].shape[0]
    local = _split_dot(jnp.concatenate(blocks, axis=0), u)
    return [local[g * rows:(g + 1) * rows] for g in range(len(blocks))]


def _new_token_mask(inclusive, n_new):
    t = lax.broadcasted_iota(jnp.int32, (N_HEADS, T_PAD, PAGE_SIZE), 1)
    s = lax.broadcasted_iota(jnp.int32, (N_HEADS, T_PAD, PAGE_SIZE), 2)
    return jnp.logical_and((s <= t) if inclusive else (s < t), s < n_new)


def _sb_block(q, kb, vb, carry, acc, mask, u):
    z = _scores(q, kb)
    ls = _log_sigmoid(z)
    lk = ls - z
    if mask is not None:
        lk = jnp.where(mask, lk, 0.0)
    blocks = _lane_blocks(lk.reshape(N_HEADS * T_PAD, -1), PAGE_SIZE)
    local = _suffix_within_pages(blocks, u)
    parts = [None] * len(blocks)
    for g in reversed(range(len(blocks))):
        parts[g] = local[g] + carry
        carry = carry + jnp.sum(blocks[g], axis=1, keepdims=True)
    p = jnp.exp(ls + jnp.concatenate(parts, axis=1).reshape(ls.shape))
    if mask is not None:
        p = jnp.where(mask, p, 0.0)
    return carry, acc + _weighted(p, vb)


def _sb_sample_kernel(pt_ref, q_ref, a_ref, b_ref, *rest, n_new, resume):
    kp = rest[:PAGES_PER_STEP]
    vp = rest[PAGES_PER_STEP:2 * PAGES_PER_STEP]
    carry_out, acc_out, carry_ref, acc_ref = rest[2 * PAGES_PER_STEP:]
    j = pl.program_id(1)
    q = q_ref[...]
    u = _suffix_matrix(PAGE_SIZE)

    @pl.when(j == 0)
    def _():
        if resume:
            carry, acc = a_ref[...], b_ref[...]
        else:
            carry, acc = _sb_block(q, a_ref[...], b_ref[...], jnp.zeros((N_HEADS * T_PAD, 1), F32),
                                   jnp.zeros((N_HEADS, T_PAD, HEAD_DIM), F32),
                                   _new_token_mask(False, n_new), u)
        carry_ref[...] = carry
        acc_ref[...] = acc

    @pl.when(jnp.max(carry_ref[...]) > -EXP_CUTOFF)
    def _():
        carry, acc = _sb_block(q, _pages_heads(kp), _pages_heads(vp), carry_ref[...], acc_ref[...], None, u)
        carry_ref[...] = carry
        acc_ref[...] = acc

    @pl.when(j == pl.num_programs(1) - 1)
    def _():
        carry_out[...] = carry_ref[...]
        acc_out[...] = acc_ref[...]


def _fox_block(q, kb, vb, lf_pages, carry, m, l, acc, mask, u):
    local = _suffix_within_pages(lf_pages, u)
    later = jnp.zeros((N_HEADS, 1), F32)
    parts = [None] * len(lf_pages)
    for g in reversed(range(len(lf_pages))):
        parts[g] = local[g] + later
        later = later + jnp.sum(lf_pages[g], axis=1, keepdims=True)
    suffix = jnp.concatenate(parts, axis=1)
    logits = _scores(q, kb) + (carry + suffix[:, None, :])
    if mask is not None:
        logits = jnp.where(mask, logits, -jnp.inf)
    m_new = jnp.maximum(m, jnp.max(logits, axis=-1, keepdims=True))
    alpha = jnp.exp(m - m_new)
    p = jnp.exp(logits - m_new)
    l = alpha * l + jnp.sum(p, axis=-1, keepdims=True)
    acc = alpha * acc + _weighted(p, vb)
    return carry + later[:, None, :], m_new, l, acc


def _fox_sample_kernel(pt_ref, q_ref, kn_ref, vn_ref, lfn_ref, *rest, n_new):
    kp = rest[:PAGES_PER_STEP]
    vp = rest[PAGES_PER_STEP:2 * PAGES_PER_STEP]
    lfp = rest[2 * PAGES_PER_STEP:3 * PAGES_PER_STEP]
    o_ref, carry_ref, m_ref, l_ref, acc_ref = rest[3 * PAGES_PER_STEP:]
    j = pl.program_id(1)
    q = q_ref[...]
    u = _suffix_matrix(PAGE_SIZE)

    @pl.when(j == 0)
    def _():
        t = lax.broadcasted_iota(jnp.int32, (N_HEADS, T_PAD, PAGE_SIZE), 1)
        s = lax.broadcasted_iota(jnp.int32, (N_HEADS, T_PAD, PAGE_SIZE), 2)
        lfn = lfn_ref[...]
        suffix_new = _split_dot(lfn, u)
        carry0 = -jnp.sum(jnp.where(s == t, suffix_new[:, None, :], 0.0), axis=-1, keepdims=True)
        carry, m, l, acc = _fox_block(q, kn_ref[...], vn_ref[...], [lfn], carry0,
                                      jnp.full((N_HEADS, T_PAD, 1), -jnp.inf, F32),
                                      jnp.zeros((N_HEADS, T_PAD, 1), F32),
                                      jnp.zeros((N_HEADS, T_PAD, HEAD_DIM), F32),
                                      _new_token_mask(True, n_new), u)
        carry_ref[...] = carry
        m_ref[...] = m
        l_ref[...] = l
        acc_ref[...] = acc

    carry, m, l, acc = _fox_block(q, _pages_heads(kp), _pages_heads(vp), [r[...] for r in lfp],
                                  carry_ref[...], m_ref[...], l_ref[...], acc_ref[...], None, u)
    carry_ref[...] = carry
    m_ref[...] = m
    l_ref[...] = l
    acc_ref[...] = acc

    @pl.when(j == pl.num_programs(1) - 1)
    def _():
        o_ref[...] = (acc / l).astype(o_ref.dtype)


def _sample_attention(page_table, q, k_new, v_new, k_cache, v_cache, layer, *, n_new,
                      lf_new=None, lf_cache=None, first_step=0, n_steps=None, resume=False):
    n_req, n_pages = page_table.shape
    steps = n_pages // PAGES_PER_STEP - first_step if n_steps is None else n_steps
    fox = lf_cache is not None

    def page_map(g):
        def index(b, j, pt):
            return (layer, pt[b, n_pages - (first_step + j + 1) * PAGES_PER_STEP + g], 0, 0)
        return index

    small = lambda b, j, pt: (b, 0, 0, 0)
    carry_spec = pl.BlockSpec((None, N_HEADS * T_PAD, 1), lambda b, j, pt: (b, 0, 0))
    state_spec = pl.BlockSpec((None, N_HEADS, T_PAD, HEAD_DIM), small)
    new_spec = pl.BlockSpec((None, N_HEADS, PAGE_SIZE, HEAD_DIM), small)
    in_specs = [state_spec] + ([carry_spec, state_spec] if resume else [new_spec, new_spec])
    args = [q, k_new, v_new]
    if fox:
        in_specs.append(pl.BlockSpec((None, N_HEADS, PAGE_SIZE), lambda b, j, pt: (b, 0, 0)))
        args.append(lf_new)
    page_rows = PAGE_SIZE * N_HEADS
    for cache in (k_cache, v_cache):
        for g in range(PAGES_PER_STEP):
            in_specs.append(pl.BlockSpec((None, None, page_rows, HEAD_DIM), page_map(g)))
            args.append(cache)
    if fox:
        for g in range(PAGES_PER_STEP):
            in_specs.append(pl.BlockSpec((None, None, N_HEADS, PAGE_SIZE), page_map(g)))
            args.append(lf_cache)
    if fox:
        scratch = [pltpu.VMEM((N_HEADS, T_PAD, 1), F32)] * 3
    else:
        scratch = [pltpu.VMEM((N_HEADS * T_PAD, 1), F32)]
    scratch.append(pltpu.VMEM((N_HEADS, T_PAD, HEAD_DIM), F32))
    nbytes = 3 * PAGES_PER_STEP * _nbytes((page_rows, HEAD_DIM), F32)
    if fox:
        kern = functools.partial(_fox_sample_kernel, n_new=n_new)
        out_specs = state_spec
        out_shape = jax.ShapeDtypeStruct((n_req, N_HEADS, T_PAD, HEAD_DIM), BF16)
    else:
        kern = functools.partial(_sb_sample_kernel, n_new=n_new, resume=resume)
        out_specs = [carry_spec, state_spec]
        out_shape = [jax.ShapeDtypeStruct((n_req, N_HEADS * T_PAD, 1), F32),
                     jax.ShapeDtypeStruct((n_req, N_HEADS, T_PAD, HEAD_DIM), F32)]
    return pl.pallas_call(
        kern,
        grid_spec=pltpu.PrefetchScalarGridSpec(
            num_scalar_prefetch=1,
            grid=(n_req, steps),
            in_specs=in_specs,
            out_specs=out_specs,
            scratch_shapes=scratch),
        out_shape=out_shape,
        compiler_params=pltpu.CompilerParams(dimension_semantics=("parallel", "arbitrary"),
                                             vmem_limit_bytes=_vmem(nbytes)),
    )(page_table, *args)


def _softmax_rows(s):
    m = jnp.max(s, axis=-1, keepdims=True)
    p = jnp.exp(s - m)
    return p, jnp.sum(p, axis=-1, keepdims=True)


def _mem_prompt_kernel(q_ref, k_ref, v_ref, o_ref):
    for h in range(MEM_HEADS):
        cols = slice(h * HEAD_DIM, (h + 1) * HEAD_DIM)
        s = lax.dot_general(q_ref[:, cols], k_ref[:, cols], (((1,), (1,)), ((), ())),
                            preferred_element_type=F32)
        p, l = _softmax_rows(s)
        o = jnp.dot(p.astype(BF16), v_ref[:, cols], preferred_element_type=F32)
        o_ref[:, cols] = (o / l).astype(o_ref.dtype)


def _mem_prompt(q, mk, mv, *, tm):
    m = q.shape[0]
    nbytes = 2 * _nbytes((tm, MEM_W), BF16) + 4 * _nbytes((tm, MEM_TOKENS), F32)
    return pl.pallas_call(
        _mem_prompt_kernel,
        grid=(m // tm,),
        in_specs=[pl.BlockSpec((tm, MEM_W), lambda i: (i, 0)),
                  pl.BlockSpec((MEM_TOKENS, MEM_W), lambda i: (0, 0)),
                  pl.BlockSpec((MEM_TOKENS, MEM_W), lambda i: (0, 0))],
        out_specs=pl.BlockSpec((tm, MEM_W), lambda i: (i, 0)),
        out_shape=jax.ShapeDtypeStruct((m, MEM_W), BF16),
        compiler_params=pltpu.CompilerParams(dimension_semantics=("parallel",),
                                             vmem_limit_bytes=_vmem(nbytes)),
    )(q, mk, mv)


def _mem_sample_kernel(q_ref, k_ref, v_ref, o_ref):
    for h in range(MEM_HEADS):
        cols = slice(h * HEAD_DIM, (h + 1) * HEAD_DIM)
        kh = k_ref[pl.ds(h, MEM_TOKENS, stride=MEM_HEADS), :].astype(BF16)
        vh = v_ref[pl.ds(h, MEM_TOKENS, stride=MEM_HEADS), :].astype(BF16)
        s = lax.dot_general(q_ref[:, cols], kh, (((1,), (1,)), ((), ())), preferred_element_type=F32)
        p, l = _softmax_rows(s)
        o = jnp.dot(p.astype(BF16), vh, preferred_element_type=F32)
        o_ref[:, cols] = (o / l).astype(o_ref.dtype)


def _mem_sample(q, k_cache, v_cache, layer):
    n_req = q.shape[0]
    rows = MEM_TOKENS * MEM_HEADS
    cache_spec = pl.BlockSpec((None, None, rows, HEAD_DIM), lambda b: (layer, b, 0, 0))
    return pl.pallas_call(
        _mem_sample_kernel,
        grid=(n_req,),
        in_specs=[pl.BlockSpec((None, T_PAD, MEM_W), lambda b: (b, 0, 0)), cache_spec, cache_spec],
        out_specs=pl.BlockSpec((None, T_PAD, MEM_W), lambda b: (b, 0, 0)),
        out_shape=jax.ShapeDtypeStruct((n_req, T_PAD, MEM_W), BF16),
        compiler_params=pltpu.CompilerParams(dimension_semantics=("parallel",)),
    )(q, k_cache, v_cache)


def _layer_weights(l, w_in, b_forget, gm_ln_g, gm_ln_b, gm_w_s, gm_b_s, w_br_gm, w_br_sb, w_br_fox,
                   w_mix_out, ln1_g, ln1_b, mem_w_q, mem_w_k, mem_w_v, mem_w_out, ln2_g, ln2_b,
                   ffn_w_gate, ffn_w_up, ffn_w_down, ln3_g, ln3_b):
    o_fg = 8 * BRANCH_W
    o_gate = o_fg + N_HEADS
    row = lambda a: a[l].reshape(1, -1).astype(F32)
    stacked = lambda a: (a.astype(BF16), l, False)
    w_in_t = jnp.swapaxes(w_in, 1, 2).astype(BF16)
    pad_fg = jnp.zeros((DEPTH, LANES - N_HEADS, D_MODEL), BF16)
    return dict(
        w_main=(w_in_t, l, True),
        w_fg=(jnp.concatenate([w_in_t[:, o_fg:o_gate], pad_fg], axis=1), l, True),
        b_fg=jnp.concatenate([b_forget[l].astype(F32), jnp.zeros((LANES - N_HEADS,), F32)]).reshape(1, LANES),
        w_gate=(w_in_t[:, o_gate:], l, True),
        gm_ln_g=row(gm_ln_g), gm_ln_b=row(gm_ln_b),
        gm_w_s=gm_w_s[l].astype(F32), gm_b_s=gm_b_s[l].astype(F32),
        w_br_gm=stacked(w_br_gm), w_br_sb=stacked(w_br_sb),
        w_br_fox=stacked(w_br_fox), w_mix_out=stacked(w_mix_out),
        ln1_g=row(ln1_g), ln1_b=row(ln1_b),
        mem_w_q=stacked(mem_w_q), mem_w_k=stacked(mem_w_k),
        mem_w_v=stacked(mem_w_v), mem_w_out=stacked(mem_w_out),
        ln2_g=row(ln2_g), ln2_b=row(ln2_b),
        ffn_w_gate=stacked(ffn_w_gate), ffn_w_up=stacked(ffn_w_up), ffn_w_down=stacked(ffn_w_down),
        ln3_g=row(ln3_g), ln3_b=row(ln3_b),
    )


def _pair(x, handle, col_block=None):
    w, layer, transposed = handle
    return (x, w, layer, col_block, transposed)


def _in_proj(x_bf, p, kv_stacked, *, tm):
    tn = 512
    seg = BRANCH_W // tn
    main = p['w_main']
    u, = _mm([_pair(x_bf, main)], _epi_gelu, [BF16], tm=tm, tn=tn, n_cols=BRANCH_W, sub_cols=SUB)
    v, = _mm([_pair(x_bf, main, lambda j: j + 1)], _epi_gelu_ln, [F32], tm=min(tm, 512), tn=BRANCH_W,
             n_cols=BRANCH_W, sub_rows=SUB, extras=[('row', p['gm_ln_g']), ('row', p['gm_ln_b'])])
    qq, = _mm([_pair(x_bf, main, lambda j: jnp.where(j < seg, j + 2 * seg, j + 4 * seg))],
              _epi_scale, [BF16], tm=tm, tn=tn, n_cols=2 * BRANCH_W, sub_cols=SUB)
    kv_stacked = kv_stacked or [None] * 4
    kv = [_kv_proj(x_bf, main, segment, prev, tm=tm)
          for segment, prev in zip((3, 4, 6, 7), kv_stacked)]
    logf, = _mm([_pair(x_bf, p['w_fg'])], _epi_logsig, [F32], tm=tm, tn=LANES, extras=[('row', p['b_fg'])])
    gates, = _mm([_pair(x_bf, p['w_gate'])], _epi_sigmoid, [BF16], tm=tm, tn=tn, sub_cols=SUB)
    return u, v, qq, [a for a, _ in kv], [b for _, b in kv], logf, gates


def _post_mixers(x, y_gm, y_sb, y_fx, gates, p, *, tm):
    gate_blocks = D_MODEL // 512
    h, = _mm([_pair(y_gm, p['w_br_gm']), _pair(y_sb, p['w_br_sb']), _pair(y_fx, p['w_br_fox'])],
             _epi_merge, [BF16], tm=min(tm, 512), tn=512, sub_cols=SUB,
             extras=[('tile', gates, 0), ('tile', gates, gate_blocks), ('tile', gates, 2 * gate_blocks)])
    return _mm([_pair(h, p['w_mix_out'])], _epi_resid_ln, [F32, BF16], tm=min(tm, 512), tn=D_MODEL,
               sub_rows=SUB, extras=[('tile', x, 0), ('row', p['ln1_g']), ('row', p['ln1_b'])])


def _mem_query(x1_bf, p, *, tm):
    q, = _mm([_pair(x1_bf, p['mem_w_q'])], _epi_scale, [BF16], tm=tm, tn=MEM_W)
    return q


def _post_mem(x1, o, p, *, tm):
    return _mm([_pair(o, p['mem_w_out'])], _epi_resid_ln, [F32, BF16], tm=min(tm, 512), tn=D_MODEL,
               sub_rows=SUB, extras=[('tile', x1, 0), ('row', p['ln2_g']), ('row', p['ln2_b'])])


def _ffn(x2, x2_bf, p, *, tm):
    hid, = _mm([_pair(x2_bf, p['ffn_w_gate']), _pair(x2_bf, p['ffn_w_up'])], _epi_swiglu, [BF16],
               tm=tm, tn=512, sub_cols=SUB)
    return _mm([_pair(hid, p['ffn_w_down'])], _epi_resid_ln, [F32, BF16], tm=min(tm, 256), tn=D_MODEL,
               sub_rows=SUB // 2, extras=[('tile', x2, 0), ('row', p['ln3_g']), ('row', p['ln3_b'])])


def _heads_first(a, n_req, t, pad_to, dtype):
    a = a.reshape(n_req, t, N_HEADS, HEAD_DIM).transpose(0, 2, 1, 3).astype(dtype)
    return jnp.pad(a, ((0, 0), (0, 0), (0, pad_to - t), (0, 0)))


def kernel(x_prompt, x_sample, cache_sb_k, cache_sb_v, cache_fox_k, cache_fox_v, cache_fox_logf,
           cache_mem_k, cache_mem_v, page_table, mem_prompt, w_in, b_forget, gm_ln_g, gm_ln_b,
           gm_w_s, gm_b_s, w_br_gm, w_br_sb, w_br_fox, w_mix_out, ln1_g, ln1_b, mem_w_q, mem_w_k,
           mem_w_v, mem_w_out, ln2_g, ln2_b, ffn_w_gate, ffn_w_up, ffn_w_down, ln3_g, ln3_b):
    batch, seq, _ = x_prompt.shape
    n_req, t_new, _ = x_sample.shape
    assert batch == 1 and t_new <= T_PAD
    n_pool = cache_sb_k.shape[1]
    rows_s = n_req * t_new
    bw = BRANCH_W

    page_rows = PAGE_SIZE * N_HEADS
    pool_view = lambda c: c.reshape(DEPTH, n_pool, page_rows, HEAD_DIM)
    c_sb_k, c_sb_v, c_fx_k, c_fx_v = map(pool_view, (cache_sb_k, cache_sb_v, cache_fox_k, cache_fox_v))
    c_fx_lf = cache_fox_logf.astype(F32).transpose(0, 1, 3, 2)
    mem_view = lambda c: c.reshape(DEPTH, n_req, MEM_TOKENS * MEM_HEADS, HEAD_DIM)
    c_mem_k, c_mem_v = mem_view(cache_mem_k), mem_view(cache_mem_v)
    mem_bf = mem_prompt.reshape(MEM_TOKENS, D_MODEL).astype(BF16)

    xp = x_prompt.reshape(seq, D_MODEL)
    xs = x_sample.reshape(rows_s, D_MODEL)
    xp_bf, xs_bf = xp.astype(BF16), xs.astype(BF16)
    outs = {k: [] for k in ('fxf_p', 'mk_p', 'mv_p', 'gmv_s', 'fxf_s')}
    kv_p = kv_s = None
    tm_p, tm_s, tq, tk = 1024, rows_s, 512, 512
    for l in range(DEPTH):
        p = _layer_weights(l, w_in, b_forget, gm_ln_g, gm_ln_b, gm_w_s, gm_b_s, w_br_gm, w_br_sb,
                           w_br_fox, w_mix_out, ln1_g, ln1_b, mem_w_q, mem_w_k, mem_w_v, mem_w_out,
                           ln2_g, ln2_b, ffn_w_gate, ffn_w_up, ffn_w_down, ln3_g, ln3_b)

        u, v, qq, kv_p, kv_bf, logf, gates = _in_proj(xp_bf, p, kv_p, tm=tm_p)
        y_gm = _gmlp(u, v, p['gm_w_s'], p['gm_b_s'].reshape(N_HEADS, GM_CHUNK, 1), tm=tm_p)
        y_sb = _sb_prompt(qq, kv_bf[0], kv_bf[1], q_off=0, tq=tq, tk=tk, sub=256)
        y_fx = _fox_prompt(qq, kv_bf[2], kv_bf[3], _neg_cumsum(logf), q_off=N_HEADS, tq=tq, tk=tk)
        x1, x1_bf = _post_mixers(xp, y_gm, y_sb, y_fx, gates, p, tm=tm_p)
        mk, mk_bf = _mm([_pair(mem_bf, p['mem_w_k'])], _epi_dual, [F32, BF16], tm=MEM_TOKENS, tn=MEM_W)
        mv, mv_bf = _mm([_pair(mem_bf, p['mem_w_v'])], _epi_dual, [F32, BF16], tm=MEM_TOKENS, tn=MEM_W)
        o_mem = _mem_prompt(_mem_query(x1_bf, p, tm=tm_p), mk_bf, mv_bf, tm=512)
        x2, x2_bf = _post_mem(x1, o_mem, p, tm=tm_p)
        xp, xp_bf = _ffn(x2, x2_bf, p, tm=tm_p)
        outs['fxf_p'].append(logf[:, :N_HEADS].reshape(batch, seq, N_HEADS))
        outs['mk_p'].append(mk.reshape(batch, MEM_TOKENS, MEM_HEADS, HEAD_DIM))
        outs['mv_p'].append(mv.reshape(batch, MEM_TOKENS, MEM_HEADS, HEAD_DIM))

        u, v, qq, kv_s, kv_bf, logf, gates = _in_proj(xs_bf, p, kv_s, tm=tm_s)
        ws_s = jnp.tile(p['gm_w_s'][:, :t_new, :t_new], (1, GM_CHUNK // t_new, GM_CHUNK // t_new))
        bs_s = jnp.tile(p['gm_b_s'][:, :t_new], (1, GM_CHUNK // t_new)).reshape(N_HEADS, GM_CHUNK, 1)
        y_gm = _gmlp(u, v, ws_s, bs_s, tm=tm_s, rows_per_request=t_new)
        hf = functools.partial(_heads_first, n_req=n_req, t=t_new, dtype=BF16)
        q_sb, q_fx = hf(qq[:, :bw], pad_to=T_PAD), hf(qq[:, bw:], pad_to=T_PAD)
        kn_sb, vn_sb, kn_fx, vn_fx = (hf(a, pad_to=PAGE_SIZE) for a in kv_bf)
        lf_s = logf[:, :N_HEADS].reshape(n_req, t_new, N_HEADS)
        lf_new = jnp.pad(lf_s.transpose(0, 2, 1), ((0, 0), (0, 0), (0, PAGE_SIZE - t_new)))
        sb_rest = functools.partial(_sample_attention, page_table, q_sb, k_cache=c_sb_k, v_cache=c_sb_v,
                                    layer=l, n_new=t_new, first_step=1, resume=True)
        sb_carry, sb_acc = _sample_attention(page_table, q_sb, kn_sb, vn_sb, c_sb_k, c_sb_v, l,
                                             n_new=t_new, n_steps=1)
        if page_table.shape[1] > PAGES_PER_STEP:
            sb_carry, sb_acc = lax.cond(jnp.max(sb_carry) > -EXP_CUTOFF,
                                        lambda c, a: tuple(sb_rest(c, a)), lambda c, a: (c, a),
                                        sb_carry, sb_acc)
        o_sb = sb_acc.astype(BF16)
        o_fx = _sample_attention(page_table, q_fx, kn_fx, vn_fx, c_fx_k, c_fx_v, l, n_new=t_new,
                                 lf_new=lf_new, lf_cache=c_fx_lf)
        rows = lambda o: o[:, :, :t_new].transpose(0, 2, 1, 3).reshape(rows_s, bw)
        x1, x1_bf = _post_mixers(xs, y_gm, rows(o_sb), rows(o_fx), gates, p, tm=tm_s)
        q_mem = _mem_query(x1_bf, p, tm=tm_s).reshape(n_req, t_new, MEM_W)
        q_mem = jnp.pad(q_mem, ((0, 0), (0, T_PAD - t_new), (0, 0)))
        o_mem = _mem_sample(q_mem, c_mem_k, c_mem_v, l)[:, :t_new].reshape(rows_s, MEM_W)
        x2, x2_bf = _post_mem(x1, o_mem, p, tm=tm_s)
        xs, xs_bf = _ffn(x2, x2_bf, p, tm=tm_s)
        outs['gmv_s'].append(v.reshape(n_req, t_new, bw))
        outs['fxf_s'].append(lf_s)

    st = jnp.stack
    kv_p = [a.reshape(DEPTH, batch, seq, N_HEADS, HEAD_DIM) for a in kv_p]
    kv_s = [a.reshape(DEPTH, n_req, t_new, N_HEADS, HEAD_DIM) for a in kv_s]
    return (xp.reshape(batch, seq, D_MODEL), xs.reshape(n_req, t_new, D_MODEL),
            *kv_p, st(outs['fxf_p']), st(outs['mk_p']), st(outs['mv_p']), st(outs['gmv_s']),
            *kv_s, st(outs['fxf_s']))
```
